```python
import math
import jax, jax.numpy as jnp
from jax import lax
import numpy as np

D_MODEL = 2048
BATCH = 4
SEQ = 4096
DEPTH = 1

D_MIX = D_MODEL
D_CONV = D_MIX // 2
D_ATT = D_MIX - D_CONV
HEAD_DIM = 64
N_HEADS = D_ATT // HEAD_DIM
CONV_WIDTH = 31
IDX_HEADS = 16
IDX_DIM = 64
TOPK_MAX = 256
Q_BLOCK = 128
REL_BUCKETS = 32
REL_MAX_DIST = 128
PEER_HEADS = 8
PEER_KEYS = 128
PEER_N = PEER_KEYS * PEER_KEYS
PEER_DKEY = 256
PEER_TOPK = 16
PEER_BLOCK = 128
LN_EPS = 1e-5
ALPHA = (2.0 * DEPTH) ** 0.25
BETA = (8.0 * DEPTH) ** -0.25

COL_GLU = 2 * D_CONV
COL_Q = D_ATT
COL_K = D_ATT
COL_V = D_ATT
COL_QIDX = IDX_HEADS * IDX_DIM
COL_KIDX = IDX_DIM
COL_WIDX = IDX_HEADS
D_IN = COL_GLU + COL_Q + COL_K + COL_V + COL_QIDX + COL_KIDX + COL_WIDX
SPLITS = (COL_GLU,
          COL_GLU + COL_Q,
          COL_GLU + COL_Q + COL_K,
          COL_GLU + COL_Q + COL_K + COL_V,
          COL_GLU + COL_Q + COL_K + COL_V + COL_QIDX,
          COL_GLU + COL_Q + COL_K + COL_V + COL_QIDX + COL_KIDX)

kernel_name = "hymba_conformer_dsa_peer_deepnorm"


def layer_norm(x, g, b):
    xf = x.astype(jnp.float32)
    mu = jnp.mean(xf, axis=-1, keepdims=True)
    var = jnp.mean(jnp.square(xf - mu), axis=-1, keepdims=True)
    return ((xf - mu) * lax.rsqrt(var + LN_EPS) * g.astype(jnp.float32)
            + b.astype(jnp.float32)).astype(x.dtype)


def rel_bucket(dist):
    max_exact = REL_BUCKETS // 2
    d = jnp.maximum(dist, 0)
    large = max_exact + (jnp.log(jnp.maximum(d, 1).astype(jnp.float32) / max_exact)
                         / math.log(REL_MAX_DIST / max_exact)
                         * (REL_BUCKETS - max_exact)).astype(jnp.int32)
    large = jnp.minimum(large, REL_BUCKETS - 1)
    return jnp.where(d < max_exact, d, large)


def conv_group(a, dw_w, dw_b, cn_g, cn_b):
    h = a[..., :D_CONV] * jax.nn.sigmoid(a[..., D_CONV:])
    h = lax.conv_general_dilated(
        h, dw_w[:, None, :].astype(h.dtype), window_strides=(1,),
        padding=[(CONV_WIDTH - 1, 0)],
        dimension_numbers=('NWC', 'WIO', 'NWC'),
        feature_group_count=D_CONV) + dw_b
    h = layer_norm(h, cn_g, cn_b)
    return jax.nn.silu(h)


def dsa_attention(q, k, v, q_idx, k_idx, w_idx, rel_bias):
    B, S = q.shape[0], q.shape[1]
    n_sel = min(TOPK_MAX, S // 4)
    n_blocks = S // Q_BLOCK
    key_pos = jnp.arange(S, dtype=jnp.int32)
    scale = HEAD_DIM ** -0.5

    def block(i):
        t0 = i * Q_BLOCK
        qb = lax.dynamic_slice_in_dim(q, t0, Q_BLOCK, axis=1)
        qib = lax.dynamic_slice_in_dim(q_idx, t0, Q_BLOCK, axis=1)
        wb = lax.dynamic_slice_in_dim(w_idx, t0, Q_BLOCK, axis=1)
        q_pos = t0 + jnp.arange(Q_BLOCK, dtype=jnp.int32)
        causal = key_pos[None, :] <= q_pos[:, None]
        rel = jax.nn.relu(jnp.einsum('bthd,bsd->bths', qib, k_idx))
        score = jnp.einsum('bths,bth->bts', rel, wb).astype(jnp.float32)
        score = jnp.where(causal[None], score, -jnp.inf)
        _, sel = lax.top_k(score, n_sel)
        kg = jax.vmap(lambda kk, ii: kk[ii])(k, sel)
        vg = jax.vmap(lambda vv, ii: vv[ii])(v, sel)
        logits = jnp.einsum('bthd,btkhd->bhtk', qb, kg).astype(jnp.float32) * scale
        dist = q_pos[None, :, None] - sel
        bias = rel_bias.astype(jnp.float32)[rel_bucket(dist)]
        logits = logits + jnp.transpose(bias, (0, 3, 1, 2))
        logits = jnp.where((dist >= 0)[:, None], logits, -jnp.inf)
        p = jax.nn.softmax(logits, axis=-1).astype(v.dtype)
        return jnp.einsum('bhtk,btkhd->bthd', p, vg)

    out = lax.map(block, jnp.arange(n_blocks, dtype=jnp.int32))
    return jnp.transpose(out, (1, 0, 2, 3, 4)).reshape(B, S, N_HEADS * HEAD_DIM)


def peer(x, w_q, sub_keys, u_tab, v_tab):
    B, S, D = x.shape
    n_tok = B * S
    xt = x.reshape(n_tok, D)
    q = (xt @ w_q).reshape(n_tok, PEER_HEADS, 2, PEER_DKEY // 2)
    s = jnp.einsum('nhcd,hckd->nhck', q, sub_keys).astype(jnp.float32)
    s1, i1 = lax.top_k(s[:, :, 0], PEER_TOPK)
    s2, i2 = lax.top_k(s[:, :, 1], PEER_TOPK)
    cand_s = (s1[..., :, None] + s2[..., None, :]).reshape(n_tok, PEER_HEADS, PEER_TOPK * PEER_TOPK)
    cand_i = (i1[..., :, None] * PEER_KEYS + i2[..., None, :]).reshape(n_tok, PEER_HEADS, PEER_TOPK * PEER_TOPK)
    top_s, pos = lax.top_k(cand_s, PEER_TOPK)
    experts = jnp.take_along_axis(cand_i, pos, axis=-1)
    gates = jax.nn.softmax(top_s, axis=-1).astype(x.dtype)
    n_blocks = n_tok // PEER_BLOCK

    def block(i):
        t0 = i * PEER_BLOCK
        xb = lax.dynamic_slice_in_dim(xt, t0, PEER_BLOCK, axis=0)
        eb = lax.dynamic_slice_in_dim(experts, t0, PEER_BLOCK, axis=0)
        gb = lax.dynamic_slice_in_dim(gates, t0, PEER_BLOCK, axis=0)
        ub = u_tab[eb]
        h = jax.nn.gelu(jnp.einsum('td,thed->the', xb, ub), approximate=False)
        vb = v_tab[eb]
        return jnp.einsum('the,thed->td', gb * h, vb)

    out = lax.map(block, jnp.arange(n_blocks, dtype=jnp.int32))
    return out.reshape(B, S, D)


def setup_inputs(seed: int = 0) -> dict:
    key = jax.random.key(seed)
    ks = jax.random.split(key, 24)
    f32 = jnp.float32
    sd = D_MODEL ** -0.5

    def nrm(k, shape, s):
        return jax.random.normal(k, shape, f32) * s

    x = jax.random.normal(ks[0], (BATCH, SEQ, D_MODEL), f32)
    w_in = jnp.concatenate([
        nrm(ks[1], (DEPTH, D_MODEL, COL_GLU), sd),
        nrm(ks[2], (DEPTH, D_MODEL, COL_Q), sd),
        nrm(ks[3], (DEPTH, D_MODEL, COL_K), sd),
        nrm(ks[4], (DEPTH, D_MODEL, COL_V), sd * BETA),
        nrm(ks[5], (DEPTH, D_MODEL, COL_QIDX), sd),
        nrm(ks[6], (DEPTH, D_MODEL, COL_KIDX), sd),
        nrm(ks[7], (DEPTH, D_MODEL, COL_WIDX), sd),
    ], axis=-1)
    w_out = nrm(ks[8], (DEPTH, D_MIX, D_MODEL), BETA * D_MIX ** -0.5)
    b_out = nrm(ks[9], (DEPTH, D_MODEL), 0.01)
    dw_w = nrm(ks[10], (DEPTH, CONV_WIDTH, D_CONV), CONV_WIDTH ** -0.5)
    dw_b = nrm(ks[11], (DEPTH, D_CONV), 0.01)
    conv_ln_g = 1.0 + nrm(ks[12], (DEPTH, D_CONV), 0.01)
    conv_ln_b = nrm(ks[13], (DEPTH, D_CONV), 0.01)
    rel_bias = nrm(ks[14], (REL_BUCKETS, N_HEADS), 0.5)
    ln1_g = 1.0 + nrm(ks[15], (DEPTH, D_MODEL), 0.01)
    ln1_b = nrm(ks[16], (DEPTH, D_MODEL), 0.01)
    peer_wq = nrm(ks[17], (DEPTH, D_MODEL, PEER_HEADS * PEER_DKEY), sd)
    peer_sub_keys = nrm(ks[18], (DEPTH, PEER_HEADS, 2, PEER_KEYS, PEER_DKEY // 2), (PEER_DKEY // 2) ** -0.5)
    peer_u = nrm(ks[19], (DEPTH, PEER_N, D_MODEL), sd)
    peer_v = nrm(ks[20], (DEPTH, PEER_N, D_MODEL), BETA)
    ln2_g = 1.0 + nrm(ks[21], (DEPTH, D_MODEL), 0.01)
    ln2_b = nrm(ks[22], (DEPTH, D_MODEL), 0.01)
    return {"x": x, "w_in": w_in, "w_out": w_out, "b_out": b_out,
            "dw_w": dw_w, "dw_b": dw_b, "conv_ln_g": conv_ln_g, "conv_ln_b": conv_ln_b,
            "rel_bias": rel_bias, "ln1_g": ln1_g, "ln1_b": ln1_b,
            "peer_wq": peer_wq, "peer_sub_keys": peer_sub_keys,
            "peer_u": peer_u, "peer_v": peer_v, "ln2_g": ln2_g, "ln2_b": ln2_b}


def reference(x, w_in, w_out, b_out, dw_w, dw_b, conv_ln_g, conv_ln_b, rel_bias,
              ln1_g, ln1_b, peer_wq, peer_sub_keys, peer_u, peer_v, ln2_g, ln2_b):
    B, S, _ = x.shape
    for l in range(DEPTH):
        proj = x @ w_in[l]
        a_glu, q, k, v, q_idx, k_idx, w_idx = jnp.split(proj, SPLITS, axis=-1)
        conv_out = conv_group(a_glu, dw_w[l], dw_b[l], conv_ln_g[l], conv_ln_b[l])
        att_out = dsa_attention(
            q.reshape(B, S, N_HEADS, HEAD_DIM),
            k.reshape(B, S, N_HEADS, HEAD_DIM),
            v.reshape(B, S, N_HEADS, HEAD_DIM),
            q_idx.reshape(B, S, IDX_HEADS, IDX_DIM), k_idx, w_idx, rel_bias)
        mix = jnp.concatenate([conv_out, att_out], axis=-1) @ w_out[l] + b_out[l]
        x = layer_norm(ALPHA * x + mix, ln1_g[l], ln1_b[l])
        y = peer(x, peer_wq[l], peer_sub_keys[l], peer_u[l], peer_v[l])
        x = layer_norm(ALPHA * x + y, ln2_g[l], ln2_b[l])
    return x
```

```python
import functools
import math

import numpy as np
import jax
import jax.numpy as jnp
from jax import lax
from jax.experimental import pallas as pl
from jax.experimental.pallas import tpu as pltpu

HEAD_DIM = 64
IDX_DIM = 64
TOPK_MAX = 256
REL_MAX_DIST = 128
PEER_TOPK = 16
LN_EPS = 1e-5

LANES = 128
SUBLANES = 8
VMEM_LIMIT_BYTES = 56 * 1024 * 1024
NEG_BIG = -1e30
INT_MIN = -(2 ** 31)

_BF16 = jnp.bfloat16
_F32 = jnp.float32
_NT = (((1,), (1,)), ((), ()))
_TN = (((0,), (0,)), ((), ()))


def _params(*sem):
    return pltpu.CompilerParams(dimension_semantics=sem, vmem_limit_bytes=VMEM_LIMIT_BYTES)


def _matmul_kernel(x_ref, w_ref, o_ref, xb_ref):
    @pl.when(pl.program_id(1) == 0)
    def _():
        xb_ref[...] = x_ref[...].astype(_BF16)

    o_ref[...] = jnp.dot(xb_ref[...], w_ref[...], preferred_element_type=_F32).astype(o_ref.dtype)


def _matmul(x, w, out_dtype, tm, tn):
    m, k = x.shape
    n = w.shape[1]
    assert m % tm == 0 and n % tn == 0
    return pl.pallas_call(
        _matmul_kernel,
        grid=(m // tm, n // tn),
        in_specs=[pl.BlockSpec((tm, k), lambda i, j: (i, 0)),
                  pl.BlockSpec((k, tn), lambda i, j: (0, j))],
        out_specs=pl.BlockSpec((tm, tn), lambda i, j: (i, j)),
        out_shape=jax.ShapeDtypeStruct((m, n), out_dtype),
        scratch_shapes=[pltpu.VMEM((tm, k), _BF16)],
        compiler_params=_params("parallel", "arbitrary"),
        name="proj_matmul",
    )(x, w)


CONV_HALO = 32
CONV_ROWS = 16


def _conv_kernel(a1_ref, a2_ref, h1_ref, h2_ref, w_ref, b_ref, g_ref, beta_ref, o_ref, hs_ref, rot_ref,
                 *, t_rows, width):
    first = pl.program_id(1) == 0
    halo = h1_ref[...] * jax.nn.sigmoid(h2_ref[...])
    hs_ref[0:CONV_HALO, :] = jnp.where(first, 0.0, halo)
    hs_ref[CONV_HALO:CONV_HALO + t_rows, :] = a1_ref[...] * jax.nn.sigmoid(a2_ref[...])
    n_rot = t_rows + CONV_HALO - SUBLANES
    rot_ref[0, :, :] = hs_ref[...]
    for r in range(1, SUBLANES):
        rot_ref[r, 0:n_rot, :] = hs_ref[r:r + n_rot, :]
    base = CONV_HALO - (width - 1)
    inv_c = 1.0 / o_ref.shape[-1]

    def chunk(c, carry):
        row0 = pl.multiple_of(c * CONV_ROWS, CONV_ROWS)
        acc = jnp.broadcast_to(b_ref[...], (CONV_ROWS, o_ref.shape[-1]))
        for j in range(width):
            q, r = divmod(base + j, SUBLANES)
            tap = rot_ref[r, pl.ds(row0 + q * SUBLANES, CONV_ROWS), :]
            acc = acc + w_ref[j:j + 1, :] * tap
        mu = jnp.sum(acc, axis=-1, keepdims=True) * inv_c
        d = acc - mu
        var = jnp.sum(d * d, axis=-1, keepdims=True) * inv_c
        y = d * lax.rsqrt(var + LN_EPS) * g_ref[...] + beta_ref[...]
        o_ref[pl.ds(row0, CONV_ROWS), :] = (y * jax.nn.sigmoid(y)).astype(o_ref.dtype)
        return carry

    lax.fori_loop(0, t_rows // CONV_ROWS, chunk, 0)


def _conv_group(proj1, dw_w, dw_b, cn_g, cn_b, batch, seq, d_conv, t_rows):
    width = dw_w.shape[0]
    assert width - 1 <= CONV_HALO and seq % t_rows == 0 and t_rows % CONV_HALO == 0
    nt = seq // t_rows
    hb = t_rows // CONV_HALO

    def cur(col):
        return pl.BlockSpec((t_rows, d_conv), lambda b, s: (b * nt + s, col))

    def halo(col):
        return pl.BlockSpec((CONV_HALO, d_conv), lambda b, s: (jnp.maximum((b * nt + s) * hb - 1, 0), col))

    vec = pl.BlockSpec((1, d_conv), lambda b, s: (0, 0))
    return pl.pallas_call(
        functools.partial(_conv_kernel, t_rows=t_rows, width=width),
        grid=(batch, nt),
        in_specs=[cur(0), cur(1), halo(0), halo(1),
                  pl.BlockSpec((width, d_conv), lambda b, s: (0, 0)), vec, vec, vec],
        out_specs=pl.BlockSpec((t_rows, d_conv), lambda b, s: (b * nt + s, 0)),
        out_shape=jax.ShapeDtypeStruct((batch * seq, d_conv), _BF16),
        scratch_shapes=[pltpu.VMEM((t_rows + CONV_HALO, d_conv), _F32),
                        pltpu.VMEM((SUBLANES, t_rows + CONV_HALO, d_conv), _F32)],
        compiler_params=_params("parallel", "arbitrary"),
        name="conv_group",
    )(proj1, proj1, proj1, proj1, dw_w, dw_b.reshape(1, -1), cn_g.reshape(1, -1), cn_b.reshape(1, -1))


ATT_TILE = 256


def _sortable_key(x):
    b = pltpu.bitcast(x, jnp.int32)
    return b ^ ((b >> 31) & jnp.int32(0x7FFFFFFF))


def _fold8(x, op):
    r, c = x.shape
    x = x.reshape(r // SUBLANES, SUBLANES, c)
    return jnp.sum(x, axis=0) if op == "sum" else jnp.max(x, axis=0)


def _dsa_kernel(q_ref, k_ref, v_ref, qi_ref, kw_ref, kwq_ref, tb_ref, o_ref,
                kib_ref, keys_ref, maskb_ref, lg_ref, mp_ref, cut_ref,
                *, n_idx_heads, n_sel, n_pos):
    t = ATT_TILE
    i = pl.program_id(1)
    p = pl.program_id(2)
    row = lax.broadcasted_iota(jnp.int32, (t, t), 0)
    col = lax.broadcasted_iota(jnp.int32, (t, t), 1)

    def causal(kt):
        return (kt * t + row) <= (i * t + col)

    @pl.when(p == 0)
    def _select():
        @pl.when(i == 0)
        def _():
            kib_ref[...] = kw_ref[:, 0:IDX_DIM].astype(_BF16)

        w_t = kwq_ref[...].T

        def score_tile(kt, carry):
            ki = kib_ref[pl.ds(pl.multiple_of(kt * t, t), t), :]
            acc = jnp.zeros((t, t), _F32)
            for h in range(n_idx_heads):
                z = lax.dot_general(ki, qi_ref[:, h * IDX_DIM:(h + 1) * IDX_DIM], _NT,
                                    preferred_element_type=_F32)
                acc = acc + jnp.maximum(z, 0.0) * w_t[IDX_DIM + h:IDX_DIM + h + 1, :]
            acc = jnp.where(causal(kt), acc, -jnp.inf)
            keys_ref[kt] = _sortable_key(acc)
            return carry

        lax.fori_loop(0, i + 1, score_tile, 0)

        def count_ge(cand):
            def body(kt, pc):
                return pc + _fold8((keys_ref[kt] >= cand).astype(jnp.int32), "sum")
            pc = lax.fori_loop(0, i + 1, body, jnp.zeros((SUBLANES, t), jnp.int32))
            return jnp.sum(pc, axis=0, keepdims=True)

        zero = jnp.zeros((1, t), jnp.int32)
        tau0 = jnp.where(count_ge(zero) >= n_sel, zero, jnp.int32(INT_MIN))

        def bit_step(it, tau):
            cand = tau + (jnp.int32(1) << (30 - it))
            return jnp.where(count_ge(cand) >= n_sel, cand, tau)

        tau = lax.fori_loop(0, 31, bit_step, tau0)

        n_gt = count_ge(tau + 1)
        need = n_sel - n_gt
        excess = count_ge(tau) - n_gt - need
        cut_ref[...] = jnp.full((1, t), n_pos, jnp.int32)

        @pl.when(jnp.max(excess) > 0)
        def _():
            def count_eq_below(bound):
                def body(kt, pc):
                    hit = (keys_ref[kt] == tau) & ((kt * t + row) < bound)
                    return pc + _fold8(hit.astype(jnp.int32), "sum")
                pc = lax.fori_loop(0, i + 1, body, jnp.zeros((SUBLANES, t), jnp.int32))
                return jnp.sum(pc, axis=0, keepdims=True)

            def pos_step(it, pos):
                cand = pos + (jnp.int32(n_pos) >> (it + 1))
                return jnp.where(count_eq_below(cand) < need, cand, pos)

            cut_ref[...] = lax.fori_loop(0, n_pos.bit_length() - 1, pos_step, jnp.zeros((1, t), jnp.int32))

        cut = cut_ref[...]

        def mask_tile(kt, carry):
            key = keys_ref[kt]
            sel = ((key > tau) | ((key == tau) & ((kt * t + row) <= cut))) & causal(kt)
            maskb_ref[kt] = jnp.where(sel, 0.0, NEG_BIG)
            return carry

        lax.fori_loop(0, i + 1, mask_tile, 0)

    lane = lax.broadcasted_iota(jnp.int32, (t, LANES), 1)
    q_all = q_ref[...] * jnp.asarray(HEAD_DIM ** -0.5, _BF16)
    outs = []
    for hl in range(2):
        in_head = (lane >= hl * HEAD_DIM) & (lane < (hl + 1) * HEAD_DIM)
        qm = jnp.where(in_head, q_all, jnp.zeros_like(q_all))

        def logits_tile(kt, bias):
            kk = k_ref[pl.ds(pl.multiple_of(kt * t, t), t), :]
            st = lax.dot_general(kk, qm, _NT, preferred_element_type=_F32) + maskb_ref[kt]
            if bias is not None:
                st = st + bias
            lg_ref[kt] = st
            mp_ref[...] = jnp.maximum(mp_ref[...], _fold8(st, "max"))

        mp_ref[...] = jnp.full(mp_ref.shape, NEG_BIG, _F32)

        def far_tile(kt, carry):
            logits_tile(kt, None)
            return carry

        lax.fori_loop(0, jnp.maximum(i - 1, 0), far_tile, 0)

        @pl.when(i > 0)
        def _():
            logits_tile(i - 1, tb_ref[hl, 0])

        logits_tile(i, tb_ref[hl, 1])
        m = jnp.max(mp_ref[...], axis=0, keepdims=True)

        def pv_tile(kt, carry):
            o_t, lp = carry
            pr = jnp.exp(lg_ref[kt] - m)
            lp = lp + _fold8(pr, "sum")
            vv = v_ref[pl.ds(pl.multiple_of(kt * t, t), t), :]
            o_t = o_t + lax.dot_general(vv, pr.astype(_BF16), _TN, preferred_element_type=_F32)
            return o_t, lp

        o_t, lp = lax.fori_loop(0, i + 1, pv_tile,
                                (jnp.zeros((LANES, t), _F32), jnp.zeros((SUBLANES, t), _F32)))
        o_t = o_t / jnp.sum(lp, axis=0, keepdims=True)
        outs.append(o_t[hl * HEAD_DIM:(hl + 1) * HEAD_DIM, :])
    o_ref[...] = jnp.concatenate(outs, axis=0).T.astype(o_ref.dtype)


def _rel_bucket_table(n_buckets):
    d = np.arange(REL_MAX_DIST, dtype=np.int32)
    max_exact = n_buckets // 2
    ratio = np.log(np.maximum(d, 1).astype(np.float32) / np.float32(max_exact)) / np.float32(
        math.log(REL_MAX_DIST / max_exact)) * np.float32(n_buckets - max_exact)
    large = np.minimum(max_exact + ratio.astype(np.int32), n_buckets - 1)
    return np.where(d < max_exact, d, large)


def _band_bias(rel_bias):
    n_buckets = rel_bias.shape[0]
    t = ATT_TILE
    assert REL_MAX_DIST <= t
    table = _rel_bucket_table(n_buckets)
    key = np.arange(t)[:, None]
    qry = np.arange(t)[None, :]
    slots = []
    for off in (t, 0):
        dist = off + qry - key
        bucket = np.where(dist < REL_MAX_DIST, table[np.clip(dist, 0, REL_MAX_DIST - 1)], n_buckets - 1)
        slots.append(bucket)
    bucket = np.stack(slots)
    rb = rel_bias.astype(_F32)
    shifted = rb - rb[n_buckets - 1][None, :]
    return jnp.transpose(shifted[bucket], (3, 0, 1, 2))


def _dsa_attention(proj2, proj1, rel_bias, batch, seq, d_att, d_conv, n_idx_heads):
    t = ATT_TILE
    n_heads = d_att // HEAD_DIM
    n_pairs = n_heads // 2
    idx_w = n_idx_heads * IDX_DIM
    assert seq % t == 0 and n_heads % 2 == 0 and (3 * d_att) % idx_w == 0 and (2 * d_conv) % LANES == 0
    assert IDX_DIM + n_idx_heads <= LANES
    nq = seq // t
    n_sel = min(TOPK_MAX, seq // 4)
    assert n_sel <= t and seq & (seq - 1) == 0
    cb = d_att // LANES
    kwb = 2 * d_conv // LANES
    tb = _band_bias(rel_bias)
    return pl.pallas_call(
        functools.partial(_dsa_kernel, n_idx_heads=n_idx_heads, n_sel=n_sel, n_pos=seq),
        grid=(batch, nq, n_pairs),
        in_specs=[pl.BlockSpec((t, LANES), lambda b, i, p: (b * nq + i, p)),
                  pl.BlockSpec((seq, LANES), lambda b, i, p: (b, cb + p)),
                  pl.BlockSpec((seq, LANES), lambda b, i, p: (b, 2 * cb + p)),
                  pl.BlockSpec((t, idx_w), lambda b, i, p: (b * nq + i, 3 * d_att // idx_w)),
                  pl.BlockSpec((seq, LANES), lambda b, i, p: (b, kwb)),
                  pl.BlockSpec((t, LANES), lambda b, i, p: (b * nq + i, kwb)),
                  pl.BlockSpec((2, 2, t, t), lambda b, i, p: (p, 0, 0, 0))],
        out_specs=pl.BlockSpec((t, LANES), lambda b, i, p: (b * nq + i, p)),
        out_shape=jax.ShapeDtypeStruct((batch * seq, d_att), _BF16),
        scratch_shapes=[pltpu.VMEM((seq, IDX_DIM), _BF16),
                        pltpu.VMEM((nq, t, t), jnp.int32),
                        pltpu.VMEM((nq, t, t), _F32),
                        pltpu.VMEM((nq, t, t), _F32),
                        pltpu.VMEM((SUBLANES, t), _F32),
                        pltpu.VMEM((1, t), jnp.int32)],
        compiler_params=_params("parallel", "arbitrary", "arbitrary"),
        name="dsa_attention",
    )(proj2, proj2, proj2, proj2, proj1, proj1, tb)


def _layer_norm(y, g, b):
    mu = jnp.mean(y, axis=-1, keepdims=True)
    d = y - mu
    var = jnp.mean(d * d, axis=-1, keepdims=True)
    return d * lax.rsqrt(var + LN_EPS) * g + b


def _outproj_kernel(c_ref, a_ref, w1_ref, w2_ref, b_ref, x_ref, g_ref, beta_ref, o_ref, ob_ref, *, alpha):
    mix = jnp.dot(c_ref[...], w1_ref[...], preferred_element_type=_F32)
    mix = mix + jnp.dot(a_ref[...], w2_ref[...], preferred_element_type=_F32) + b_ref[...]
    y = _layer_norm(alpha * x_ref[...] + mix, g_ref[...], beta_ref[...])
    o_ref[...] = y
    ob_ref[...] = y.astype(_BF16)


def _out_projection(conv_out, att_out, w_out, b_out, x2, g, beta, alpha, tm):
    n, d = x2.shape
    dc = conv_out.shape[1]
    da = att_out.shape[1]
    vec = pl.BlockSpec((1, d), lambda i: (0, 0))
    return pl.pallas_call(
        functools.partial(_outproj_kernel, alpha=alpha),
        grid=(n // tm,),
        in_specs=[pl.BlockSpec((tm, dc), lambda i: (i, 0)),
                  pl.BlockSpec((tm, da), lambda i: (i, 0)),
                  pl.BlockSpec((dc, d), lambda i: (0, 0)),
                  pl.BlockSpec((da, d), lambda i: (0, 0)),
                  vec,
                  pl.BlockSpec((tm, d), lambda i: (i, 0)),
                  vec, vec],
        out_specs=[pl.BlockSpec((tm, d), lambda i: (i, 0)), pl.BlockSpec((tm, d), lambda i: (i, 0))],
        out_shape=[jax.ShapeDtypeStruct((n, d), _F32), jax.ShapeDtypeStruct((n, d), _BF16)],
        compiler_params=_params("parallel"),
        name="out_projection_ln1",
    )(conv_out, att_out, w_out[:dc], w_out[dc:], b_out.reshape(1, d), x2, g.reshape(1, d), beta.reshape(1, d))


def _pop_max(s, iota, n_rows):
    mx = jnp.max(s, axis=0, keepdims=True)
    first = jnp.min(jnp.where(s == mx, iota, n_rows), axis=0, keepdims=True)
    return mx, jnp.where(iota == first, -jnp.inf, s)


def _peer_pairs(k):
    return [(i, j) for i in range(k) for j in range(k) if (i + 1) * (j + 1) <= k]


def _peer_select_kernel(x_ref, wq_ref, sk_ref, s_ref, e_ref, th_ref, *, n_heads, n_keys):
    tm = x_ref.shape[0]
    dk = sk_ref.shape[2]
    q = jnp.dot(x_ref[...], wq_ref[...], preferred_element_type=_F32).astype(_BF16)
    iota_k = lax.broadcasted_iota(jnp.int32, (n_keys, tm), 0)
    pairs = _peer_pairs(PEER_TOPK)
    n_cand = -(-len(pairs) // SUBLANES) * SUBLANES
    iota_c = lax.broadcasted_iota(jnp.int32, (n_cand, tm), 0)
    for h in range(n_heads):
        tops = []
        scores = []
        for c in range(2):
            g = 2 * h + c
            s_t = lax.dot_general(sk_ref[g], q[:, g * dk:(g + 1) * dk], _NT,
                                  preferred_element_type=_F32)
            s_ref[g] = s_t
            scores.append(s_t)
            rows = []
            rest = s_t
            for _ in range(PEER_TOPK):
                mx, rest = _pop_max(rest, iota_k, n_keys)
                rows.append(mx)
            tops.append(rows)
        a, b = tops
        cand = [a[i] + b[j] for (i, j) in pairs]
        cand += [jnp.full((1, tm), -jnp.inf, _F32)] * (n_cand - len(pairs))
        rest = jnp.concatenate(cand, axis=0)
        top_m = a[0] + b[0]
        z = jnp.zeros((1, tm), _F32)
        theta = top_m
        for _ in range(PEER_TOPK):
            theta, rest = _pop_max(rest, iota_c, n_cand)
            z = z + jnp.exp(theta - top_m)
        th_ref[h:h + 1, :] = theta
        e_ref[2 * h] = jnp.exp(scores[0] - a[0])
        e_ref[2 * h + 1] = jnp.exp(scores[1] - b[0]) / z


def _peer_select(x1b, wq, sub_keys, tm):
    n, d = x1b.shape
    n_heads, _, n_keys, dk = sub_keys.shape
    sk = sub_keys.reshape(2 * n_heads, n_keys, dk).astype(_BF16)
    g2 = 2 * n_heads
    assert n_heads % SUBLANES == 0 or n_heads == SUBLANES
    return pl.pallas_call(
        functools.partial(_peer_select_kernel, n_heads=n_heads, n_keys=n_keys),
        grid=(n // tm,),
        in_specs=[pl.BlockSpec((tm, d), lambda i: (i, 0)),
                  pl.BlockSpec((d, g2 * dk), lambda i: (0, 0)),
                  pl.BlockSpec((g2, n_keys, dk), lambda i: (0, 0, 0))],
        out_specs=[pl.BlockSpec((g2, n_keys, tm), lambda i: (0, 0, i)),
                   pl.BlockSpec((g2, n_keys, tm), lambda i: (0, 0, i)),
                   pl.BlockSpec((n_heads, tm), lambda i: (0, i))],
        out_shape=[jax.ShapeDtypeStruct((g2, n_keys, n), _F32),
                   jax.ShapeDtypeStruct((g2, n_keys, n), _F32),
                   jax.ShapeDtypeStruct((n_heads, n), _F32)],
        compiler_params=_params("parallel"),
        name="peer_select",
    )(x1b, wq, sk)


def _peer_expert_kernel(xb_ref, u_ref, vt_ref, s_ref, e_ref, th_ref, x_ref, g_ref, beta_ref, o_ref, y_ref,
                        *, n_heads, n_keys, alpha):
    k = pl.program_id(1)
    te = u_ref.shape[0]

    @pl.when(k == 0)
    def _():
        y_ref[...] = jnp.zeros_like(y_ref)

    h_t = lax.dot_general(u_ref[...], xb_ref[...], _NT, preferred_element_type=_F32)
    act = 0.5 * h_t * (1.0 + lax.erf(h_t * (2.0 ** -0.5)))
    blocks = []
    for r in range(te // n_keys):
        i1 = k * (te // n_keys) + r
        gate = jnp.zeros((n_keys, h_t.shape[1]), _F32)
        for h in range(n_heads):
            s1 = s_ref[2 * h, pl.ds(i1, 1), :]
            e1 = e_ref[2 * h, pl.ds(i1, 1), :]
            sel = (s_ref[2 * h + 1] + s1) >= th_ref[h:h + 1, :]
            gate = gate + jnp.where(sel, e_ref[2 * h + 1] * e1, 0.0)
        blocks.append((gate * act[r * n_keys:(r + 1) * n_keys, :]).astype(_BF16))
    a_t = jnp.concatenate(blocks, axis=0) if len(blocks) > 1 else blocks[0]
    y_ref[...] += jnp.dot(vt_ref[...], a_t, preferred_element_type=_F32)

    @pl.when(k == pl.num_programs(1) - 1)
    def _():
        y = alpha * x_ref[...] + y_ref[...].T
        o_ref[...] = _layer_norm(y, g_ref[...], beta_ref[...])


def _peer_experts(x1, x1b, u_b, vt_b, s_t, e_t, theta, g, beta, alpha, tm, te):
    n, d = x1.shape
    n_exp = u_b.shape[0]
    g2, n_keys, _ = s_t.shape
    n_heads = g2 // 2
    assert n_exp == n_keys * n_keys and te % n_keys == 0 and n_exp % te == 0 and n % tm == 0
    vec = pl.BlockSpec((1, d), lambda j, k: (0, 0))
    return pl.pallas_call(
        functools.partial(_peer_expert_kernel, n_heads=n_heads, n_keys=n_keys, alpha=alpha),
        grid=(n // tm, n_exp // te),
        in_specs=[pl.BlockSpec((tm, d), lambda j, k: (j, 0)),
                  pl.BlockSpec((te, d), lambda j, k: (k, 0)),
                  pl.BlockSpec((d, te), lambda j, k: (0, k)),
                  pl.BlockSpec((g2, n_keys, tm), lambda j, k: (0, 0, j)),
                  pl.BlockSpec((g2, n_keys, tm), lambda j, k: (0, 0, j)),
                  pl.BlockSpec((n_heads, tm), lambda j, k: (0, j)),
                  pl.BlockSpec((tm, d), lambda j, k: (j, 0)),
                  vec, vec],
        out_specs=pl.BlockSpec((tm, d), lambda j, k: (j, 0)),
        out_shape=jax.ShapeDtypeStruct((n, d), _F32),
        scratch_shapes=[pltpu.VMEM((d, tm), _F32)],
        compiler_params=_params("parallel", "arbitrary"),
        name="peer_experts_ln2",
    )(x1b, u_b, vt_b, s_t, e_t, theta, x1, g.reshape(1, d), beta.reshape(1, d))


def _pick(n, pref):
    t = min(pref, n)
    while n % t:
        t //= 2
    return t


def kernel(x, w_in, w_out, b_out, dw_w, dw_b, conv_ln_g, conv_ln_b, rel_bias, ln1_g, ln1_b,
           peer_wq, peer_sub_keys, peer_u, peer_v, ln2_g, ln2_b):
    batch, seq, d = x.shape
    depth = w_in.shape[0]
    alpha = (2.0 * depth) ** 0.25
    d_conv = dw_w.shape[2]
    d_att = d - d_conv
    d_in = w_in.shape[2]
    n_idx_heads = (d_in - 2 * d_conv - 3 * d_att - IDX_DIM) // (IDX_DIM + 1)
    idx_w = n_idx_heads * IDX_DIM
    assert 2 * d_conv + 3 * d_att + idx_w + IDX_DIM + n_idx_heads == d_in
    n = batch * seq
    x2 = x.reshape(n, d)
    for l in range(depth):
        c0 = 2 * d_conv
        c1 = c0 + 3 * d_att + idx_w
        tail = jnp.pad(w_in[l][:, c1:], ((0, 0), (0, LANES - (d_in - c1))))
        w1 = jnp.concatenate([w_in[l][:, :c0], tail], axis=1).astype(_BF16)
        w2 = w_in[l][:, c0:c1].astype(_BF16)
        tm = _pick(n, 1024)
        proj1 = _matmul(x2, w1, _F32, tm, w1.shape[1])
        proj2 = _matmul(x2, w2, _BF16, tm, _pick(w2.shape[1], 1024))
        conv_out = _conv_group(proj1, dw_w[l], dw_b[l], conv_ln_g[l], conv_ln_b[l], batch, seq, d_conv,
                               _pick(seq, 256))
        att_out = _dsa_attention(proj2, proj1, rel_bias, batch, seq, d_att, d_conv, n_idx_heads)
        x1, x1b = _out_projection(conv_out, att_out, w_out[l].astype(_BF16), b_out[l], x2,
                                  ln1_g[l], ln1_b[l], alpha, _pick(n, 512))
        s_t, e_t, theta = _peer_select(x1b, peer_wq[l].astype(_BF16), peer_sub_keys[l], _pick(n, 256))
        x2 = _peer_experts(x1, x1b, peer_u[l].astype(_BF16), peer_v[l].T.astype(_BF16), s_t, e_t, theta,
                           ln2_g[l], ln2_b[l], alpha, _pick(n, 512), 512)
    return x2.reshape(batch, seq, d)
```

```python
import functools
import math

import numpy as np
import jax
import jax.numpy as jnp
from jax import lax
from jax.experimental import pallas as pl
from jax.experimental.pallas import tpu as pltpu

HEAD_DIM = 64
IDX_DIM = 64
TOPK_MAX = 256
REL_MAX_DIST = 128
PEER_TOPK = 16
LN_EPS = 1e-5

LANES = 128
SUBLANES = 8
VMEM_LIMIT_BYTES = 56 * 1024 * 1024
NEG_BIG = -1e30
INT_MIN = -(2 ** 31)

_BF16 = jnp.bfloat16
_F32 = jnp.float32
_NT = (((1,), (1,)), ((), ()))
_TN = (((0,), (0,)), ((), ()))


def _params(*sem, flags=None):
    return pltpu.CompilerParams(dimension_semantics=sem, vmem_limit_bytes=VMEM_LIMIT_BYTES, flags=flags)


def _matmul_kernel(x_ref, w_ref, o_ref, xb_ref):
    @pl.when(pl.program_id(1) == 0)
    def _():
        xb_ref[...] = x_ref[...].astype(_BF16)

    o_ref[...] = jnp.dot(xb_ref[...], w_ref[...], preferred_element_type=_F32).astype(o_ref.dtype)


def _matmul(x, w, out_dtype, tm, tn):
    m, k = x.shape
    n = w.shape[1]
    assert m % tm == 0 and n % tn == 0
    return pl.pallas_call(
        _matmul_kernel,
        grid=(m // tm, n // tn),
        in_specs=[pl.BlockSpec((tm, k), lambda i, j: (i, 0)),
                  pl.BlockSpec((k, tn), lambda i, j: (0, j))],
        out_specs=pl.BlockSpec((tm, tn), lambda i, j: (i, j)),
        out_shape=jax.ShapeDtypeStruct((m, n), out_dtype),
        scratch_shapes=[pltpu.VMEM((tm, k), _BF16)],
        compiler_params=_params("parallel", "arbitrary"),
        name="proj_matmul",
    )(x, w)


CONV_HALO = 32
CONV_ROWS = 16


def _conv_kernel(a1_ref, a2_ref, h1_ref, h2_ref, w_ref, b_ref, g_ref, beta_ref, o_ref, hs_ref, rot_ref,
                 *, t_rows, width):
    first = pl.program_id(1) == 0
    halo = h1_ref[...] * jax.nn.sigmoid(h2_ref[...])
    hs_ref[0:CONV_HALO, :] = jnp.where(first, 0.0, halo)
    hs_ref[CONV_HALO:CONV_HALO + t_rows, :] = a1_ref[...] * jax.nn.sigmoid(a2_ref[...])
    n_rot = t_rows + CONV_HALO - SUBLANES
    rot_ref[0, :, :] = hs_ref[...]
    for r in range(1, SUBLANES):
        rot_ref[r, 0:n_rot, :] = hs_ref[r:r + n_rot, :]
    base = CONV_HALO - (width - 1)
    inv_c = 1.0 / o_ref.shape[-1]

    def chunk(c, carry):
        row0 = pl.multiple_of(c * CONV_ROWS, CONV_ROWS)
        acc = jnp.broadcast_to(b_ref[...], (CONV_ROWS, o_ref.shape[-1]))
        for j in range(width):
            q, r = divmod(base + j, SUBLANES)
            tap = rot_ref[r, pl.ds(row0 + q * SUBLANES, CONV_ROWS), :]
            acc = acc + w_ref[j:j + 1, :] * tap
        mu = jnp.sum(acc, axis=-1, keepdims=True) * inv_c
        d = acc - mu
        var = jnp.sum(d * d, axis=-1, keepdims=True) * inv_c
        y = d * lax.rsqrt(var + LN_EPS) * g_ref[...] + beta_ref[...]
        o_ref[pl.ds(row0, CONV_ROWS), :] = (y * jax.nn.sigmoid(y)).astype(o_ref.dtype)
        return carry

    lax.fori_loop(0, t_rows // CONV_ROWS, chunk, 0)


def _conv_group(proj1, dw_w, dw_b, cn_g, cn_b, batch, seq, d_conv, t_rows):
    width = dw_w.shape[0]
    assert width - 1 <= CONV_HALO and seq % t_rows == 0 and t_rows % CONV_HALO == 0
    nt = seq // t_rows
    hb = t_rows // CONV_HALO

    def cur(col):
        return pl.BlockSpec((t_rows, d_conv), lambda b, s: (b * nt + s, col))

    def halo(col):
        return pl.BlockSpec((CONV_HALO, d_conv), lambda b, s: (jnp.maximum((b * nt + s) * hb - 1, 0), col))

    vec = pl.BlockSpec((1, d_conv), lambda b, s: (0, 0))
    return pl.pallas_call(
        functools.partial(_conv_kernel, t_rows=t_rows, width=width),
        grid=(batch, nt),
        in_specs=[cur(0), cur(1), halo(0), halo(1),
                  pl.BlockSpec((width, d_conv), lambda b, s: (0, 0)), vec, vec, vec],
        out_specs=pl.BlockSpec((t_rows, d_conv), lambda b, s: (b * nt + s, 0)),
        out_shape=jax.ShapeDtypeStruct((batch * seq, d_conv), _BF16),
        scratch_shapes=[pltpu.VMEM((t_rows + CONV_HALO, d_conv), _F32),
                        pltpu.VMEM((SUBLANES, t_rows + CONV_HALO, d_conv), _F32)],
        compiler_params=_params("parallel", "arbitrary"),
        name="conv_group",
    )(proj1, proj1, proj1, proj1, dw_w, dw_b.reshape(1, -1), cn_g.reshape(1, -1), cn_b.reshape(1, -1))


ATT_TILE = 256


def _sortable_key(x):
    b = pltpu.bitcast(x, jnp.int32)
    return b ^ ((b >> 31) & jnp.int32(0x7FFFFFFF))


def _fold8(x, op):
    r, c = x.shape
    x = x.reshape(r // SUBLANES, SUBLANES, c)
    return jnp.sum(x, axis=0) if op == "sum" else jnp.max(x, axis=0)


def _dsa_kernel(q_ref, k_ref, v_ref, qi_ref, kw_ref, kwq_ref, tb_ref, o_ref,
                kib_ref, keys_ref, maskb_ref, lg_ref, mp_ref, cut_ref,
                *, n_idx_heads, n_sel, n_pos):
    t = ATT_TILE
    i = pl.program_id(1)
    p = pl.program_id(2)
    row = lax.broadcasted_iota(jnp.int32, (t, t), 0)
    col = lax.broadcasted_iota(jnp.int32, (t, t), 1)

    def causal(kt):
        return (kt * t + row) <= (i * t + col)

    @pl.when(p == 0)
    def _select():
        @pl.when(i == 0)
        def _():
            kib_ref[...] = kw_ref[:, 0:IDX_DIM].astype(_BF16)

        w_t = kwq_ref[...].T

        def score_tile(kt, carry):
            ki = kib_ref[pl.ds(pl.multiple_of(kt * t, t), t), :]
            acc = jnp.zeros((t, t), _F32)
            for h in range(n_idx_heads):
                z = lax.dot_general(ki, qi_ref[:, h * IDX_DIM:(h + 1) * IDX_DIM], _NT,
                                    preferred_element_type=_F32)
                acc = acc + jnp.maximum(z, 0.0) * w_t[IDX_DIM + h:IDX_DIM + h + 1, :]
            acc = jnp.where(causal(kt), acc, -jnp.inf)
            keys_ref[kt] = _sortable_key(acc)
            return carry

        lax.fori_loop(0, i + 1, score_tile, 0)

        def count_ge(cand):
            def body(kt, pc):
                return pc + _fold8((keys_ref[kt] >= cand).astype(jnp.int32), "sum")
            pc = lax.fori_loop(0, i + 1, body, jnp.zeros((SUBLANES, t), jnp.int32))
            return jnp.sum(pc, axis=0, keepdims=True)

        zero = jnp.zeros((1, t), jnp.int32)
        tau0 = jnp.where(count_ge(zero) >= n_sel, zero, jnp.int32(INT_MIN))

        def bit_step(it, tau):
            cand = tau + (jnp.int32(1) << (30 - it))
            return jnp.where(count_ge(cand) >= n_sel, cand, tau)

        tau = lax.fori_loop(0, 31, bit_step, tau0)

        n_gt = count_ge(tau + 1)
        need = n_sel - n_gt
        excess = count_ge(tau) - n_gt - need
        cut_ref[...] = jnp.full((1, t), n_pos, jnp.int32)

        @pl.when(jnp.max(excess) > 0)
        def _():
            def count_eq_below(bound):
                def body(kt, pc):
                    hit = (keys_ref[kt] == tau) & ((kt * t + row) < bound)
                    return pc + _fold8(hit.astype(jnp.int32), "sum")
                pc = lax.fori_loop(0, i + 1, body, jnp.zeros((SUBLANES, t), jnp.int32))
                return jnp.sum(pc, axis=0, keepdims=True)

            def pos_step(it, pos):
                cand = pos + (jnp.int32(n_pos) >> (it + 1))
                return jnp.where(count_eq_below(cand) < need, cand, pos)

            cut_ref[...] = lax.fori_loop(0, n_pos.bit_length() - 1, pos_step, jnp.zeros((1, t), jnp.int32))

        cut = cut_ref[...]

        def mask_tile(kt, carry):
            key = keys_ref[kt]
            sel = ((key > tau) | ((key == tau) & ((kt * t + row) <= cut))) & causal(kt)
            maskb_ref[kt] = jnp.where(sel, 0.0, NEG_BIG)
            return carry

        lax.fori_loop(0, i + 1, mask_tile, 0)

    lane = lax.broadcasted_iota(jnp.int32, (t, LANES), 1)
    q_all = q_ref[...] * jnp.asarray(HEAD_DIM ** -0.5, _BF16)
    outs = []
    for hl in range(2):
        in_head = (lane >= hl * HEAD_DIM) & (lane < (hl + 1) * HEAD_DIM)
        qm = jnp.where(in_head, q_all, jnp.zeros_like(q_all))

        def logits_tile(kt, bias):
            kk = k_ref[pl.ds(pl.multiple_of(kt * t, t), t), :]
            st = lax.dot_general(kk, qm, _NT, preferred_element_type=_F32) + maskb_ref[kt]
            if bias is not None:
                st = st + bias
            lg_ref[kt] = st
            mp_ref[...] = jnp.maximum(mp_ref[...], _fold8(st, "max"))

        mp_ref[...] = jnp.full(mp_ref.shape, NEG_BIG, _F32)

        def far_tile(kt, carry):
            logits_tile(kt, None)
            return carry

        lax.fori_loop(0, jnp.maximum(i - 1, 0), far_tile, 0)

        @pl.when(i > 0)
        def _():
            logits_tile(i - 1, tb_ref[hl, 0])

        logits_tile(i, tb_ref[hl, 1])
        m = jnp.max(mp_ref[...], axis=0, keepdims=True)

        def pv_tile(kt, carry):
            o_t, lp = carry
            pr = jnp.exp(lg_ref[kt] - m)
            lp = lp + _fold8(pr, "sum")
            vv = v_ref[pl.ds(pl.multiple_of(kt * t, t), t), :]
            o_t = o_t + lax.dot_general(vv, pr.astype(_BF16), _TN, preferred_element_type=_F32)
            return o_t, lp

        o_t, lp = lax.fori_loop(0, i + 1, pv_tile,
                                (jnp.zeros((LANES, t), _F32), jnp.zeros((SUBLANES, t), _F32)))
        o_t = o_t / jnp.sum(lp, axis=0, keepdims=True)
        outs.append(o_t[hl * HEAD_DIM:(hl + 1) * HEAD_DIM, :])
    o_ref[...] = jnp.concatenate(outs, axis=0).T.astype(o_ref.dtype)


def _rel_bucket_table(n_buckets):
    d = np.arange(REL_MAX_DIST, dtype=np.int32)
    max_exact = n_buckets // 2
    ratio = np.log(np.maximum(d, 1).astype(np.float32) / np.float32(max_exact)) / np.float32(
        math.log(REL_MAX_DIST / max_exact)) * np.float32(n_buckets - max_exact)
    large = np.minimum(max_exact + ratio.astype(np.int32), n_buckets - 1)
    return np.where(d < max_exact, d, large)


def _band_bias(rel_bias):
    n_buckets, n_heads = rel_bias.shape
    t = ATT_TILE
    assert REL_MAX_DIST <= t
    rb = rel_bias.astype(_F32)
    shifted = rb - rb[n_buckets - 1][None, :]
    by_dist = shifted[_rel_bucket_table(n_buckets)].T
    dv = jnp.concatenate([jnp.zeros((n_heads, t - 1), _F32), by_dist,
                          jnp.zeros((n_heads, 2 * t - REL_MAX_DIST + 1), _F32)], axis=1)
    slots = []
    for off in (t, 0):
        v = dv[:, off:off + 2 * t]
        flat = jnp.tile(v, (1, t))[:, t - 1:t - 1 + t * (2 * t - 1)]
        slots.append(flat.reshape(n_heads, t, 2 * t - 1)[:, :, :t])
    return jnp.stack(slots, axis=1)


def _dsa_attention(proj2, proj1, rel_bias, batch, seq, d_att, d_conv, n_idx_heads):
    t = ATT_TILE
    n_heads = d_att // HEAD_DIM
    n_pairs = n_heads // 2
    idx_w = n_idx_heads * IDX_DIM
    assert seq % t == 0 and n_heads % 2 == 0 and (3 * d_att) % idx_w == 0 and (2 * d_conv) % LANES == 0
    assert IDX_DIM + n_idx_heads <= LANES
    nq = seq // t
    n_sel = min(TOPK_MAX, seq // 4)
    assert n_sel <= t and seq & (seq - 1) == 0
    cb = d_att // LANES
    kwb = 2 * d_conv // LANES
    tb = _band_bias(rel_bias)
    return pl.pallas_call(
        functools.partial(_dsa_kernel, n_idx_heads=n_idx_heads, n_sel=n_sel, n_pos=seq),
        grid=(batch, nq, n_pairs),
        in_specs=[pl.BlockSpec((t, LANES), lambda b, i, p: (b * nq + i, p)),
                  pl.BlockSpec((seq, LANES), lambda b, i, p: (b, cb + p)),
                  pl.BlockSpec((seq, LANES), lambda b, i, p: (b, 2 * cb + p)),
                  pl.BlockSpec((t, idx_w), lambda b, i, p: (b * nq + i, 3 * d_att // idx_w)),
                  pl.BlockSpec((seq, LANES), lambda b, i, p: (b, kwb)),
                  pl.BlockSpec((t, LANES), lambda b, i, p: (b * nq + i, kwb)),
                  pl.BlockSpec((2, 2, t, t), lambda b, i, p: (p, 0, 0, 0))],
        out_specs=pl.BlockSpec((t, LANES), lambda b, i, p: (b * nq + i, p)),
        out_shape=jax.ShapeDtypeStruct((batch * seq, d_att), _BF16),
        scratch_shapes=[pltpu.VMEM((seq, IDX_DIM), _BF16),
                        pltpu.VMEM((nq, t, t), jnp.int32),
                        pltpu.VMEM((nq, t, t), _F32),
                        pltpu.VMEM((nq, t, t), _F32),
                        pltpu.VMEM((SUBLANES, t), _F32),
                        pltpu.VMEM((1, t), jnp.int32)],
        compiler_params=_params("parallel", "arbitrary", "arbitrary"),
        name="dsa_attention",
    )(proj2, proj2, proj2, proj2, proj1, proj1, tb)


def _layer_norm(y, g, b):
    mu = jnp.mean(y, axis=-1, keepdims=True)
    d = y - mu
    var = jnp.mean(d * d, axis=-1, keepdims=True)
    return d * lax.rsqrt(var + LN_EPS) * g + b


def _outproj_kernel(c_ref, a_ref, w1_ref, w2_ref, b_ref, x_ref, g_ref, beta_ref, o_ref, ob_ref, ot_ref, *, alpha):
    mix = jnp.dot(c_ref[...], w1_ref[...], preferred_element_type=_F32)
    mix = mix + jnp.dot(a_ref[...], w2_ref[...], preferred_element_type=_F32) + b_ref[...]
    y = _layer_norm(alpha * x_ref[...] + mix, g_ref[...], beta_ref[...])
    o_ref[...] = y
    ob_ref[...] = y.astype(_BF16)
    ot_ref[...] = y.T.astype(_BF16)


def _out_projection(conv_out, att_out, w_out, b_out, x2, g, beta, alpha, tm):
    n, d = x2.shape
    dc = conv_out.shape[1]
    da = att_out.shape[1]
    vec = pl.BlockSpec((1, d), lambda i: (0, 0))
    return pl.pallas_call(
        functools.partial(_outproj_kernel, alpha=alpha),
        grid=(n // tm,),
        in_specs=[pl.BlockSpec((tm, dc), lambda i: (i, 0)),
                  pl.BlockSpec((tm, da), lambda i: (i, 0)),
                  pl.BlockSpec((dc, d), lambda i: (0, 0)),
                  pl.BlockSpec((da, d), lambda i: (0, 0)),
                  vec,
                  pl.BlockSpec((tm, d), lambda i: (i, 0)),
                  vec, vec],
        out_specs=[pl.BlockSpec((tm, d), lambda i: (i, 0)), pl.BlockSpec((tm, d), lambda i: (i, 0)),
                   pl.BlockSpec((d, tm), lambda i: (0, i))],
        out_shape=[jax.ShapeDtypeStruct((n, d), _F32), jax.ShapeDtypeStruct((n, d), _BF16),
                   jax.ShapeDtypeStruct((d, n), _BF16)],
        compiler_params=_params("parallel"),
        name="out_projection_ln1",
    )(conv_out, att_out, w_out[:dc], w_out[dc:], b_out.reshape(1, d), x2, g.reshape(1, d), beta.reshape(1, d))


def _pop_max(s, iota, n_rows):
    mx = jnp.max(s, axis=0, keepdims=True)
    first = jnp.min(jnp.where(s == mx, iota, n_rows), axis=0, keepdims=True)
    return mx, jnp.where(iota == first, -jnp.inf, s)


def _peer_pairs(k):
    return [(i, j) for i in range(k) for j in range(k) if (i + 1) * (j + 1) <= k]


PEER_POPS = PEER_TOPK + 1


def _peer_select_kernel(x_ref, wq_ref, sk_ref, s2_ref, t_ref, e1_ref, e2_ref, *, n_heads, n_keys):
    tm = x_ref.shape[0]
    dk = sk_ref.shape[2]
    q = jnp.dot(x_ref[...], wq_ref[...], preferred_element_type=_F32).astype(_BF16)
    iota_k = lax.broadcasted_iota(jnp.int32, (n_keys, tm), 0)
    pairs = _peer_pairs(PEER_POPS)
    n_cand = -(-len(pairs) // SUBLANES) * SUBLANES
    iota_c = lax.broadcasted_iota(jnp.int32, (n_cand, tm), 0)
    for h in range(n_heads):
        tops = []
        scores = []
        for c in range(2):
            g = 2 * h + c
            s_t = lax.dot_general(sk_ref[g], q[:, g * dk:(g + 1) * dk], _NT,
                                  preferred_element_type=_F32)
            scores.append(s_t)
            rows = []
            rest = s_t
            for _ in range(PEER_POPS):
                mx, rest = _pop_max(rest, iota_k, n_keys)
                rows.append(mx)
            tops.append(rows)
        a, b = tops
        cand = [a[i] + b[j] for (i, j) in pairs]
        cand += [jnp.full((1, tm), -jnp.inf, _F32)] * (n_cand - len(pairs))
        rest = jnp.concatenate(cand, axis=0)
        vals = []
        for _ in range(PEER_POPS):
            v, rest = _pop_max(rest, iota_c, n_cand)
            vals.append(v)
        z = jnp.zeros((1, tm), _F32)
        for v in vals[:PEER_TOPK]:
            z = z + jnp.exp(v - vals[0])
        theta = 0.5 * (vals[PEER_TOPK - 1] + vals[PEER_TOPK])
        s2_ref[h] = scores[1]
        t_ref[h] = theta - scores[0]
        e1_ref[h] = jnp.exp(scores[0] - a[0])
        e2_ref[h] = jnp.exp(scores[1] - b[0]) / z


def _peer_select(x1b, wq, sub_keys, tm):
    n, d = x1b.shape
    n_heads, _, n_keys, dk = sub_keys.shape
    assert n_keys >= PEER_POPS
    sk = sub_keys.reshape(2 * n_heads, n_keys, dk).astype(_BF16)
    g2 = 2 * n_heads
    out = pl.BlockSpec((n_heads, n_keys, tm), lambda i: (0, 0, i))
    shape = jax.ShapeDtypeStruct((n_heads, n_keys, n), _F32)
    return pl.pallas_call(
        functools.partial(_peer_select_kernel, n_heads=n_heads, n_keys=n_keys),
        grid=(n // tm,),
        in_specs=[pl.BlockSpec((tm, d), lambda i: (i, 0)),
                  pl.BlockSpec((d, g2 * dk), lambda i: (0, 0)),
                  pl.BlockSpec((g2, n_keys, dk), lambda i: (0, 0, 0))],
        out_specs=[out, out, out, out],
        out_shape=[shape, shape, shape, shape],
        compiler_params=_params("parallel"),
        name="peer_select",
    )(x1b, wq, sk)


PEER_SUB = 256


def _peer_expert_kernel(xt_ref, u_ref, vt_ref, s2_ref, t_ref, e1_ref, e2_ref, x_ref, g_ref, beta_ref, o_ref,
                        y_ref, a0_ref, a1_ref, *, n_heads, n_keys, alpha):
    k = pl.program_id(1)
    nk = pl.num_programs(1) - 1
    te = u_ref.shape[0]
    tm = xt_ref.shape[1]
    d = vt_ref.shape[0]
    kt = jnp.minimum(k, nk - 1)
    units = [(r, c) for r in range(te // n_keys) for c in range(tm // PEER_SUB)]
    n_pieces = d // PEER_SUB
    pieces_of = [[m for m in range(n_pieces) if m * len(units) // n_pieces == ui] for ui in range(len(units))]

    @pl.when(k == 0)
    def _():
        y_ref[...] = jnp.zeros_like(y_ref)
        a1_ref[...] = jnp.zeros_like(a1_ref)

    def body(a_cur, a_prev):
        def second_matmul_piece(m):
            rows = slice(m * PEER_SUB, (m + 1) * PEER_SUB)
            y_ref[rows, :] += jnp.dot(vt_ref[rows, :], a_prev[...], preferred_element_type=_F32)

        for ui, (r, c2) in enumerate(units):
            rows = slice(r * n_keys, (r + 1) * n_keys)
            h_t = jnp.dot(u_ref[rows, :], xt_ref[:, c2 * PEER_SUB:(c2 + 1) * PEER_SUB],
                          preferred_element_type=_F32)
            for m in pieces_of[ui]:
                second_matmul_piece(m)
            act = 0.5 * h_t * (1.0 + lax.erf(h_t * (2.0 ** -0.5)))
            i1 = kt * (te // n_keys) + r
            for c in range(PEER_SUB // LANES):
                cols = slice(c2 * PEER_SUB + c * LANES, c2 * PEER_SUB + (c + 1) * LANES)
                gate = jnp.zeros((n_keys, LANES), _F32)
                for h in range(n_heads):
                    sel = s2_ref[h, :, cols] >= t_ref[h, pl.ds(i1, 1), :][:, cols]
                    gate = gate + jnp.where(sel, e2_ref[h, :, cols] * e1_ref[h, pl.ds(i1, 1), :][:, cols], 0.0)
                a_cur[rows, cols] = (gate * act[:, c * LANES:(c + 1) * LANES]).astype(a_cur.dtype)

    @pl.when(k % 2 == 0)
    def _():
        body(a0_ref, a1_ref)

    @pl.when(k % 2 == 1)
    def _():
        body(a1_ref, a0_ref)

    @pl.when(k == nk)
    def _():
        y = alpha * x_ref[...] + y_ref[...].T
        o_ref[...] = _layer_norm(y, g_ref[...], beta_ref[...])


def _peer_experts(x1, x1t, u_b, vt_b, s2, t, e1, e2, g, beta, alpha, tm, te):
    n, d = x1.shape
    n_exp = u_b.shape[0]
    n_heads, n_keys, _ = s2.shape
    assert n_exp == n_keys * n_keys and te % n_keys == 0 and d % PEER_SUB == 0
    assert n_exp % te == 0 and n % tm == 0 and tm % PEER_SUB == 0
    nk = n_exp // te
    vec = pl.BlockSpec((1, d), lambda j, k: (0, 0))
    sel = pl.BlockSpec((n_heads, n_keys, tm), lambda j, k: (0, 0, j))
    once = dict(pipeline_mode=pl.Buffered(1))
    return pl.pallas_call(
        functools.partial(_peer_expert_kernel, n_heads=n_heads, n_keys=n_keys, alpha=alpha),
        grid=(n // tm, nk + 1),
        in_specs=[pl.BlockSpec((d, tm), lambda j, k: (0, j)),
                  pl.BlockSpec((te, d), lambda j, k: (jnp.minimum(k, nk - 1), 0)),
                  pl.BlockSpec((d, te), lambda j, k: (0, jnp.maximum(k - 1, 0))),
                  sel, sel, sel, sel,
                  pl.BlockSpec((tm, d), lambda j, k: (j, 0), **once),
                  vec, vec],
        out_specs=pl.BlockSpec((tm, d), lambda j, k: (j, 0), **once),
        out_shape=jax.ShapeDtypeStruct((n, d), _F32),
        scratch_shapes=[pltpu.VMEM((d, tm), _F32), pltpu.VMEM((te, tm), _BF16), pltpu.VMEM((te, tm), _BF16)],
        compiler_params=_params("parallel", "arbitrary"),
        name="peer_experts_ln2",
    )(x1t, u_b, vt_b, s2, t, e1, e2, x1, g.reshape(1, d), beta.reshape(1, d))


def _pick(n, pref):
    t = min(pref, n)
    while n % t:
        t //= 2
    return t


def kernel(x, w_in, w_out, b_out, dw_w, dw_b, conv_ln_g, conv_ln_b, rel_bias, ln1_g, ln1_b,
           peer_wq, peer_sub_keys, peer_u, peer_v, ln2_g, ln2_b):
    batch, seq, d = x.shape
    depth = w_in.shape[0]
    alpha = (2.0 * depth) ** 0.25
    d_conv = dw_w.shape[2]
    d_att = d - d_conv
    d_in = w_in.shape[2]
    n_idx_heads = (d_in - 2 * d_conv - 3 * d_att - IDX_DIM) // (IDX_DIM + 1)
    idx_w = n_idx_heads * IDX_DIM
    assert 2 * d_conv + 3 * d_att + idx_w + IDX_DIM + n_idx_heads == d_in
    n = batch * seq
    x2 = x.reshape(n, d)
    for l in range(depth):
        c0 = 2 * d_conv
        c1 = c0 + 3 * d_att + idx_w
        tail = jnp.pad(w_in[l][:, c1:], ((0, 0), (0, LANES - (d_in - c1))))
        w1 = jnp.concatenate([w_in[l][:, :c0], tail], axis=1).astype(_BF16)
        w2 = w_in[l][:, c0:c1].astype(_BF16)
        tm = _pick(n, 1024)
        proj1 = _matmul(x2, w1, _F32, tm, w1.shape[1])
        proj2 = _matmul(x2, w2, _BF16, tm, _pick(w2.shape[1], 1024))
        conv_out = _conv_group(proj1, dw_w[l], dw_b[l], conv_ln_g[l], conv_ln_b[l], batch, seq, d_conv,
                               _pick(seq, 256))
        att_out = _dsa_attention(proj2, proj1, rel_bias, batch, seq, d_att, d_conv, n_idx_heads)
        x1, x1b, x1t = _out_projection(conv_out, att_out, w_out[l].astype(_BF16), b_out[l], x2,
                                       ln1_g[l], ln1_b[l], alpha, _pick(n, 512))
        s2, t, e1, e2 = _peer_select(x1b, peer_wq[l].astype(_BF16), peer_sub_keys[l], _pick(n, 256))
        x2 = _peer_experts(x1, x1t, peer_u[l].astype(_BF16), peer_v[l].T.astype(_BF16), s2, t, e1, e2,
                           ln2_g[l], ln2_b[l], alpha, _pick(n, 512), 512)
    return x2.reshape(batch, seq, d)
```

```python
import functools
import math

import numpy as np
import jax
import jax.numpy as jnp
from jax import lax
from jax.experimental import pallas as pl
from jax.experimental.pallas import tpu as pltpu

HEAD_DIM = 64
IDX_DIM = 64
TOPK_MAX = 256
REL_MAX_DIST = 128
PEER_TOPK = 16
LN_EPS = 1e-5

LANES = 128
SUBLANES = 8
VMEM_LIMIT_BYTES = 56 * 1024 * 1024
NEG_BIG = -1e30
INT_MIN = -(2 ** 31)

_BF16 = jnp.bfloat16
_F32 = jnp.float32
_NT = (((1,), (1,)), ((), ()))
_TN = (((0,), (0,)), ((), ()))


def _params(*sem, flags=None):
    return pltpu.CompilerParams(dimension_semantics=sem, vmem_limit_bytes=VMEM_LIMIT_BYTES, flags=flags)


def _matmul_kernel(x_ref, w_ref, o_ref, xb_ref):
    @pl.when(pl.program_id(1) == 0)
    def _():
        xb_ref[...] = x_ref[...].astype(_BF16)

    o_ref[...] = jnp.dot(xb_ref[...], w_ref[...], preferred_element_type=_F32).astype(o_ref.dtype)


def _matmul(x, w, out_dtype, tm, tn):
    m, k = x.shape
    n = w.shape[1]
    assert m % tm == 0 and n % tn == 0
    return pl.pallas_call(
        _matmul_kernel,
        grid=(m // tm, n // tn),
        in_specs=[pl.BlockSpec((tm, k), lambda i, j: (i, 0)),
                  pl.BlockSpec((k, tn), lambda i, j: (0, j))],
        out_specs=pl.BlockSpec((tm, tn), lambda i, j: (i, j)),
        out_shape=jax.ShapeDtypeStruct((m, n), out_dtype),
        scratch_shapes=[pltpu.VMEM((tm, k), _BF16)],
        compiler_params=_params("parallel", "arbitrary"),
        name="proj_matmul",
    )(x, w)


CONV_HALO = 32
CONV_ROWS = 16


def _conv_kernel(a1_ref, a2_ref, h1_ref, h2_ref, w_ref, b_ref, g_ref, beta_ref, o_ref, hs_ref, rot_ref,
                 *, t_rows, width):
    first = pl.program_id(1) == 0
    halo = h1_ref[...] * jax.nn.sigmoid(h2_ref[...])
    hs_ref[0:CONV_HALO, :] = jnp.where(first, 0.0, halo)
    hs_ref[CONV_HALO:CONV_HALO + t_rows, :] = a1_ref[...] * jax.nn.sigmoid(a2_ref[...])
    n_rot = t_rows + CONV_HALO - SUBLANES
    rot_ref[0, :, :] = hs_ref[...]
    for r in range(1, SUBLANES):
        rot_ref[r, 0:n_rot, :] = hs_ref[r:r + n_rot, :]
    base = CONV_HALO - (width - 1)
    inv_c = 1.0 / o_ref.shape[-1]

    def chunk(c, carry):
        row0 = pl.multiple_of(c * CONV_ROWS, CONV_ROWS)
        acc = jnp.broadcast_to(b_ref[...], (CONV_ROWS, o_ref.shape[-1]))
        for j in range(width):
            q, r = divmod(base + j, SUBLANES)
            tap = rot_ref[r, pl.ds(row0 + q * SUBLANES, CONV_ROWS), :]
            acc = acc + w_ref[j:j + 1, :] * tap
        mu = jnp.sum(acc, axis=-1, keepdims=True) * inv_c
        d = acc - mu
        var = jnp.sum(d * d, axis=-1, keepdims=True) * inv_c
        y = d * lax.rsqrt(var + LN_EPS) * g_ref[...] + beta_ref[...]
        o_ref[pl.ds(row0, CONV_ROWS), :] = (y * jax.nn.sigmoid(y)).astype(o_ref.dtype)
        return carry

    lax.fori_loop(0, t_rows // CONV_ROWS, chunk, 0)


def _conv_group(proj1, dw_w, dw_b, cn_g, cn_b, batch, seq, d_conv, t_rows):
    width = dw_w.shape[0]
    assert width - 1 <= CONV_HALO and seq % t_rows == 0 and t_rows % CONV_HALO == 0
    nt = seq // t_rows
    hb = t_rows // CONV_HALO

    def cur(col):
        return pl.BlockSpec((t_rows, d_conv), lambda b, s: (b * nt + s, col))

    def halo(col):
        return pl.BlockSpec((CONV_HALO, d_conv), lambda b, s: (jnp.maximum((b * nt + s) * hb - 1, 0), col))

    vec = pl.BlockSpec((1, d_conv), lambda b, s: (0, 0))
    return pl.pallas_call(
        functools.partial(_conv_kernel, t_rows=t_rows, width=width),
        grid=(batch, nt),
        in_specs=[cur(0), cur(1), halo(0), halo(1),
                  pl.BlockSpec((width, d_conv), lambda b, s: (0, 0)), vec, vec, vec],
        out_specs=pl.BlockSpec((t_rows, d_conv), lambda b, s: (b * nt + s, 0)),
        out_shape=jax.ShapeDtypeStruct((batch * seq, d_conv), _BF16),
        scratch_shapes=[pltpu.VMEM((t_rows + CONV_HALO, d_conv), _F32),
                        pltpu.VMEM((SUBLANES, t_rows + CONV_HALO, d_conv), _F32)],
        compiler_params=_params("parallel", "arbitrary"),
        name="conv_group",
    )(proj1, proj1, proj1, proj1, dw_w, dw_b.reshape(1, -1), cn_g.reshape(1, -1), cn_b.reshape(1, -1))


ATT_TILE = 256
ATT_CHUNK = 4


def _sortable_key(x):
    b = pltpu.bitcast(x, jnp.int32)
    return b ^ ((b >> 31) & jnp.int32(0x7FFFFFFF))


def _fold8(x, op):
    r, c = x.shape
    x = x.reshape(r // SUBLANES, SUBLANES, c)
    return jnp.sum(x, axis=0) if op == "sum" else jnp.max(x, axis=0)


def _dsa_kernel(q_ref, k_ref, v_ref, qi_ref, kw_ref, kwq_ref, tb_ref, o_ref,
                kib_ref, keys_ref, maskb_ref, maskn_ref, lg_ref, lgn_ref, acc_ref, cut_ref,
                *, n_idx_heads, n_sel, n_pos):
    t = ATT_TILE
    i = pl.program_id(1)
    p = pl.program_id(2)
    near_base = jnp.maximum(i - 1, 0)
    n_far = near_base
    n_far_chunks = lax.shift_right_logical(n_far + (ATT_CHUNK - 1), ATT_CHUNK.bit_length() - 1)
    row = lax.broadcasted_iota(jnp.int32, (t, t), 0)
    col = lax.broadcasted_iota(jnp.int32, (t, t), 1)

    def causal(kt):
        return (kt * t + row) <= (i * t + col)

    @pl.when(p == 0)
    def _select():
        @pl.when(i == 0)
        def _():
            kib_ref[...] = kw_ref[:, 0:IDX_DIM].astype(_BF16)

        w_t = kwq_ref[...].T

        def score_tile(kt, carry):
            ki = kib_ref[pl.ds(pl.multiple_of(kt * t, t), t), :]
            acc = jnp.zeros((t, t), _F32)
            for h in range(n_idx_heads):
                z = lax.dot_general(ki, qi_ref[:, h * IDX_DIM:(h + 1) * IDX_DIM], _NT,
                                    preferred_element_type=_F32)
                acc = acc + jnp.maximum(z, 0.0) * w_t[IDX_DIM + h:IDX_DIM + h + 1, :]
            acc = jnp.where(causal(kt), acc, -jnp.inf)
            keys_ref[kt] = _sortable_key(acc)
            return carry

        lax.fori_loop(0, i + 1, score_tile, 0)

        def count_ge(cand):
            def body(kt, pc):
                return pc + _fold8((keys_ref[kt] >= cand).astype(jnp.int32), "sum")
            pc = lax.fori_loop(0, i + 1, body, jnp.zeros((SUBLANES, t), jnp.int32))
            return jnp.sum(pc, axis=0, keepdims=True)

        zero = jnp.zeros((1, t), jnp.int32)
        tau0 = jnp.where(count_ge(zero) >= n_sel, zero, jnp.int32(INT_MIN))

        def bit_step(it, tau):
            cand = tau + (jnp.int32(1) << (30 - it))
            return jnp.where(count_ge(cand) >= n_sel, cand, tau)

        tau = lax.fori_loop(0, 31, bit_step, tau0)

        n_gt = count_ge(tau + 1)
        need = n_sel - n_gt
        excess = count_ge(tau) - n_gt - need
        cut_ref[...] = jnp.full((1, t), n_pos, jnp.int32)

        @pl.when(jnp.max(excess) > 0)
        def _():
            def count_eq_below(bound):
                def body(kt, pc):
                    hit = (keys_ref[kt] == tau) & ((kt * t + row) < bound)
                    return pc + _fold8(hit.astype(jnp.int32), "sum")
                pc = lax.fori_loop(0, i + 1, body, jnp.zeros((SUBLANES, t), jnp.int32))
                return jnp.sum(pc, axis=0, keepdims=True)

            def pos_step(it, pos):
                cand = pos + (jnp.int32(n_pos) >> (it + 1))
                return jnp.where(count_eq_below(cand) < need, cand, pos)

            cut_ref[...] = lax.fori_loop(0, n_pos.bit_length() - 1, pos_step, jnp.zeros((1, t), jnp.int32))

        cut = cut_ref[...]

        def mask_tile(kt, carry):
            key = keys_ref[kt]
            sel = ((key > tau) | ((key == tau) & ((kt * t + row) <= cut))) & causal(kt)
            maskb_ref[kt] = jnp.where(sel, 0.0, NEG_BIG)
            return carry

        lax.fori_loop(0, i + 1, mask_tile, 0)
        neg_tile = jnp.full((t, t), NEG_BIG, _F32)
        maskn_ref[0] = maskb_ref[near_base]
        maskn_ref[1] = jnp.where(i > 0, maskb_ref[i], neg_tile)

        def pad_tile(kt, carry):
            maskb_ref[kt] = neg_tile
            return carry

        lax.fori_loop(n_far, n_far_chunks * ATT_CHUNK, pad_tile, 0)

    ck = ATT_CHUNK * t
    lane = lax.broadcasted_iota(jnp.int32, (t, LANES), 1)
    q_all = q_ref[...] * jnp.asarray(HEAD_DIM ** -0.5, _BF16)
    qm = [jnp.where((lane >= hl * HEAD_DIM) & (lane < (hl + 1) * HEAD_DIM), q_all, jnp.zeros_like(q_all))
          for hl in range(2)]
    near_rows = pl.ds(pl.multiple_of(near_base * t, t), 2 * t)
    near_bias = (jnp.where(i > 0, 0, 1), 1)

    def far_logits(c, mps):
        kk = k_ref[pl.ds(pl.multiple_of(c * ck, ck), ck), :]
        new = []
        for hl in range(2):
            st = lax.dot_general(kk, qm[hl], _NT, preferred_element_type=_F32)
            mp = mps[hl]
            for u in range(ATT_CHUNK):
                tile = st[u * t:(u + 1) * t, :] + maskb_ref[c * ATT_CHUNK + u]
                lg_ref[hl, c * ATT_CHUNK + u] = tile
                mp = jnp.maximum(mp, _fold8(tile, "max"))
            new.append(mp)
        return tuple(new)

    mp0 = jnp.full((SUBLANES, t), NEG_BIG, _F32)
    mps = lax.fori_loop(0, n_far_chunks, far_logits, (mp0, mp0))
    kk = k_ref[near_rows, :]
    m = []
    for hl in range(2):
        st = lax.dot_general(kk, qm[hl], _NT, preferred_element_type=_F32)
        mp = mps[hl]
        for s in range(2):
            tile = st[s * t:(s + 1) * t, :] + maskn_ref[s] + tb_ref[hl, near_bias[s]]
            lgn_ref[hl, s] = tile
            mp = jnp.maximum(mp, _fold8(tile, "max"))
        m.append(jnp.max(mp, axis=0, keepdims=True))

    acc_ref[...] = jnp.zeros_like(acc_ref)

    def far_pv(c, lps):
        vv = v_ref[pl.ds(pl.multiple_of(c * ck, ck), ck), :]
        new = []
        for hl in range(2):
            pr = jnp.exp(lg_ref[hl, pl.ds(c * ATT_CHUNK, ATT_CHUNK)].reshape(ck, t) - m[hl])
            new.append(lps[hl] + _fold8(pr, "sum"))
            acc_ref[hl] += lax.dot_general(vv, pr.astype(_BF16), _TN, preferred_element_type=_F32)
        return tuple(new)

    lp0 = jnp.zeros((SUBLANES, t), _F32)
    lps = lax.fori_loop(0, n_far_chunks, far_pv, (lp0, lp0))
    vv = v_ref[near_rows, :]
    outs = []
    for hl in range(2):
        pr = jnp.exp(lgn_ref[hl].reshape(2 * t, t) - m[hl])
        lp = lps[hl] + _fold8(pr, "sum")
        o_t = acc_ref[hl] + lax.dot_general(vv, pr.astype(_BF16), _TN, preferred_element_type=_F32)
        o_t = o_t / jnp.sum(lp, axis=0, keepdims=True)
        outs.append(o_t[hl * HEAD_DIM:(hl + 1) * HEAD_DIM, :])
    o_ref[...] = jnp.concatenate(outs, axis=0).T.astype(o_ref.dtype)


def _rel_bucket_table(n_buckets):
    d = np.arange(REL_MAX_DIST, dtype=np.int32)
    max_exact = n_buckets // 2
    ratio = np.log(np.maximum(d, 1).astype(np.float32) / np.float32(max_exact)) / np.float32(
        math.log(REL_MAX_DIST / max_exact)) * np.float32(n_buckets - max_exact)
    large = np.minimum(max_exact + ratio.astype(np.int32), n_buckets - 1)
    return np.where(d < max_exact, d, large)


def _band_bias(rel_bias):
    n_buckets, n_heads = rel_bias.shape
    t = ATT_TILE
    assert REL_MAX_DIST <= t
    rb = rel_bias.astype(_F32)
    shifted = rb - rb[n_buckets - 1][None, :]
    by_dist = shifted[_rel_bucket_table(n_buckets)].T
    dv = jnp.concatenate([jnp.zeros((n_heads, t - 1), _F32), by_dist,
                          jnp.zeros((n_heads, 2 * t - REL_MAX_DIST + 1), _F32)], axis=1)
    slots = []
    for off in (t, 0):
        v = dv[:, off:off + 2 * t]
        flat = jnp.tile(v, (1, t))[:, t - 1:t - 1 + t * (2 * t - 1)]
        slots.append(flat.reshape(n_heads, t, 2 * t - 1)[:, :, :t])
    return jnp.stack(slots, axis=1)


def _dsa_attention(proj2, proj1, rel_bias, batch, seq, d_att, d_conv, n_idx_heads):
    t = ATT_TILE
    n_heads = d_att // HEAD_DIM
    n_pairs = n_heads // 2
    idx_w = n_idx_heads * IDX_DIM
    assert seq % t == 0 and n_heads % 2 == 0 and (3 * d_att) % idx_w == 0 and (2 * d_conv) % LANES == 0
    assert IDX_DIM + n_idx_heads <= LANES
    nq = seq // t
    n_sel = min(TOPK_MAX, seq // 4)
    assert n_sel <= t and seq & (seq - 1) == 0 and nq >= 2 and nq % ATT_CHUNK == 0
    cb = d_att // LANES
    kwb = 2 * d_conv // LANES
    tb = _band_bias(rel_bias)
    return pl.pallas_call(
        functools.partial(_dsa_kernel, n_idx_heads=n_idx_heads, n_sel=n_sel, n_pos=seq),
        grid=(batch, nq, n_pairs),
        in_specs=[pl.BlockSpec((t, LANES), lambda b, i, p: (b * nq + i, p)),
                  pl.BlockSpec((seq, LANES), lambda b, i, p: (b, cb + p)),
                  pl.BlockSpec((seq, LANES), lambda b, i, p: (b, 2 * cb + p)),
                  pl.BlockSpec((t, idx_w), lambda b, i, p: (b * nq + i, 3 * d_att // idx_w)),
                  pl.BlockSpec((seq, LANES), lambda b, i, p: (b, kwb)),
                  pl.BlockSpec((t, LANES), lambda b, i, p: (b * nq + i, kwb)),
                  pl.BlockSpec((2, 2, t, t), lambda b, i, p: (p, 0, 0, 0))],
        out_specs=pl.BlockSpec((t, LANES), lambda b, i, p: (b * nq + i, p)),
        out_shape=jax.ShapeDtypeStruct((batch * seq, d_att), _BF16),
        scratch_shapes=[pltpu.VMEM((seq, IDX_DIM), _BF16),
                        pltpu.VMEM((nq, t, t), jnp.int32),
                        pltpu.VMEM((nq, t, t), _F32),
                        pltpu.VMEM((2, t, t), _F32),
                        pltpu.VMEM((2, nq, t, t), _F32),
                        pltpu.VMEM((2, 2, t, t), _F32),
                        pltpu.VMEM((2, LANES, t), _F32),
                        pltpu.VMEM((1, t), jnp.int32)],
        compiler_params=_params("parallel", "arbitrary", "arbitrary"),
        name="dsa_attention",
    )(proj2, proj2, proj2, proj2, proj1, proj1, tb)


def _layer_norm(y, g, b):
    mu = jnp.mean(y, axis=-1, keepdims=True)
    d = y - mu
    var = jnp.mean(d * d, axis=-1, keepdims=True)
    return d * lax.rsqrt(var + LN_EPS) * g + b


def _outproj_kernel(c_ref, a_ref, w1_ref, w2_ref, b_ref, x_ref, g_ref, beta_ref, o_ref, ob_ref, ot_ref, *, alpha):
    mix = jnp.dot(c_ref[...], w1_ref[...], preferred_element_type=_F32)
    mix = mix + jnp.dot(a_ref[...], w2_ref[...], preferred_element_type=_F32) + b_ref[...]
    y = _layer_norm(alpha * x_ref[...] + mix, g_ref[...], beta_ref[...])
    o_ref[...] = y
    ob_ref[...] = y.astype(_BF16)
    ot_ref[...] = y.T.astype(_BF16)


def _out_projection(conv_out, att_out, w_out, b_out, x2, g, beta, alpha, tm):
    n, d = x2.shape
    dc = conv_out.shape[1]
    da = att_out.shape[1]
    vec = pl.BlockSpec((1, d), lambda i: (0, 0))
    return pl.pallas_call(
        functools.partial(_outproj_kernel, alpha=alpha),
        grid=(n // tm,),
        in_specs=[pl.BlockSpec((tm, dc), lambda i: (i, 0)),
                  pl.BlockSpec((tm, da), lambda i: (i, 0)),
                  pl.BlockSpec((dc, d), lambda i: (0, 0)),
                  pl.BlockSpec((da, d), lambda i: (0, 0)),
                  vec,
                  pl.BlockSpec((tm, d), lambda i: (i, 0)),
                  vec, vec],
        out_specs=[pl.BlockSpec((tm, d), lambda i: (i, 0)), pl.BlockSpec((tm, d), lambda i: (i, 0)),
                   pl.BlockSpec((d, tm), lambda i: (0, i))],
        out_shape=[jax.ShapeDtypeStruct((n, d), _F32), jax.ShapeDtypeStruct((n, d), _BF16),
                   jax.ShapeDtypeStruct((d, n), _BF16)],
        compiler_params=_params("parallel"),
        name="out_projection_ln1",
    )(conv_out, att_out, w_out[:dc], w_out[dc:], b_out.reshape(1, d), x2, g.reshape(1, d), beta.reshape(1, d))


def _pop_max(s, iota, n_rows):
    mx = jnp.max(s, axis=0, keepdims=True)
    first = jnp.min(jnp.where(s == mx, iota, n_rows), axis=0, keepdims=True)
    return mx, jnp.where(iota == first, -jnp.inf, s)


def _peer_pairs(k):
    return [(i, j) for i in range(k) for j in range(k) if (i + 1) * (j + 1) <= k]


PEER_POPS = PEER_TOPK + 1


def _peer_select_kernel(x_ref, wq_ref, sk_ref, s2_ref, t_ref, e1_ref, e2_ref, *, n_heads, n_keys):
    tm = x_ref.shape[0]
    dk = sk_ref.shape[2]
    q = jnp.dot(x_ref[...], wq_ref[...], preferred_element_type=_F32).astype(_BF16)
    iota_k = lax.broadcasted_iota(jnp.int32, (n_keys, tm), 0)
    pairs = _peer_pairs(PEER_POPS)
    n_cand = -(-len(pairs) // SUBLANES) * SUBLANES
    iota_c = lax.broadcasted_iota(jnp.int32, (n_cand, tm), 0)
    for h in range(n_heads):
        tops = []
        scores = []
        for c in range(2):
            g = 2 * h + c
            s_t = lax.dot_general(sk_ref[g], q[:, g * dk:(g + 1) * dk], _NT,
                                  preferred_element_type=_F32)
            scores.append(s_t)
            rows = []
            rest = s_t
            for _ in range(PEER_POPS):
                mx, rest = _pop_max(rest, iota_k, n_keys)
                rows.append(mx)
            tops.append(rows)
        a, b = tops
        cand = [a[i] + b[j] for (i, j) in pairs]
        cand += [jnp.full((1, tm), -jnp.inf, _F32)] * (n_cand - len(pairs))
        rest = jnp.concatenate(cand, axis=0)
        vals = []
        for _ in range(PEER_POPS):
            v, rest = _pop_max(rest, iota_c, n_cand)
            vals.append(v)
        z = jnp.zeros((1, tm), _F32)
        for v in vals[:PEER_TOPK]:
            z = z + jnp.exp(v - vals[0])
        theta = 0.5 * (vals[PEER_TOPK - 1] + vals[PEER_TOPK])
        s2_ref[h] = scores[1]
        t_ref[h] = theta - scores[0]
        e1_ref[h] = jnp.exp(scores[0] - a[0])
        e2_ref[h] = jnp.exp(scores[1] - b[0]) / z


def _peer_select(x1b, wq, sub_keys, tm):
    n, d = x1b.shape
    n_heads, _, n_keys, dk = sub_keys.shape
    assert n_keys >= PEER_POPS
    sk = sub_keys.reshape(2 * n_heads, n_keys, dk).astype(_BF16)
    g2 = 2 * n_heads
    out = pl.BlockSpec((n_heads, n_keys, tm), lambda i: (0, 0, i))
    shape = jax.ShapeDtypeStruct((n_heads, n_keys, n), _F32)
    return pl.pallas_call(
        functools.partial(_peer_select_kernel, n_heads=n_heads, n_keys=n_keys),
        grid=(n // tm,),
        in_specs=[pl.BlockSpec((tm, d), lambda i: (i, 0)),
                  pl.BlockSpec((d, g2 * dk), lambda i: (0, 0)),
                  pl.BlockSpec((g2, n_keys, dk), lambda i: (0, 0, 0))],
        out_specs=[out, out, out, out],
        out_shape=[shape, shape, shape, shape],
        compiler_params=_params("parallel"),
        name="peer_select",
    )(x1b, wq, sk)


PEER_SUB = 256


def _peer_expert_kernel(xt_ref, u_ref, vt_ref, s2_ref, t_ref, e1_ref, e2_ref, x_ref, g_ref, beta_ref, o_ref,
                        y_ref, a0_ref, a1_ref, *, n_heads, n_keys, alpha):
    k = pl.program_id(1)
    nk = pl.num_programs(1) - 1
    te = u_ref.shape[0]
    tm = xt_ref.shape[1]
    d = vt_ref.shape[0]
    kt = jnp.minimum(k, nk - 1)
    units = [(r, c) for r in range(te // n_keys) for c in range(tm // PEER_SUB)]
    n_pieces = d // PEER_SUB
    pieces_of = [[m for m in range(n_pieces) if m * len(units) // n_pieces == ui] for ui in range(len(units))]

    @pl.when(k == 0)
    def _():
        y_ref[...] = jnp.zeros_like(y_ref)
        a1_ref[...] = jnp.zeros_like(a1_ref)

    def body(a_cur, a_prev):
        def second_matmul_piece(m):
            rows = slice(m * PEER_SUB, (m + 1) * PEER_SUB)
            y_ref[rows, :] += jnp.dot(vt_ref[rows, :], a_prev[...], preferred_element_type=_F32)

        for ui, (r, c2) in enumerate(units):
            rows = slice(r * n_keys, (r + 1) * n_keys)
            h_t = jnp.dot(u_ref[rows, :], xt_ref[:, c2 * PEER_SUB:(c2 + 1) * PEER_SUB],
                          preferred_element_type=_F32)
            for m in pieces_of[ui]:
                second_matmul_piece(m)
            act = 0.5 * h_t * (1.0 + lax.erf(h_t * (2.0 ** -0.5)))
            i1 = kt * (te // n_keys) + r
            for c in range(PEER_SUB // LANES):
                cols = slice(c2 * PEER_SUB + c * LANES, c2 * PEER_SUB + (c + 1) * LANES)
                gate = jnp.zeros((n_keys, LANES), _F32)
                for h in range(n_heads):
                    sel = s2_ref[h, :, cols] >= t_ref[h, pl.ds(i1, 1), :][:, cols]
                    gate = gate + jnp.where(sel, e2_ref[h, :, cols] * e1_ref[h, pl.ds(i1, 1), :][:, cols], 0.0)
                a_cur[rows, cols] = (gate * act[:, c * LANES:(c + 1) * LANES]).astype(a_cur.dtype)

    @pl.when(k % 2 == 0)
    def _():
        body(a0_ref, a1_ref)

    @pl.when(k % 2 == 1)
    def _():
        body(a1_ref, a0_ref)

    @pl.when(k == nk)
    def _():
        y = alpha * x_ref[...] + y_ref[...].T
        o_ref[...] = _layer_norm(y, g_ref[...], beta_ref[...])


def _peer_experts(x1, x1t, u_b, vt_b, s2, t, e1, e2, g, beta, alpha, tm, te):
    n, d = x1.shape
    n_exp = u_b.shape[0]
    n_heads, n_keys, _ = s2.shape
    assert n_exp == n_keys * n_keys and te % n_keys == 0 and d % PEER_SUB == 0
    assert n_exp % te == 0 and n % tm == 0 and tm % PEER_SUB == 0
    nk = n_exp // te
    vec = pl.BlockSpec((1, d), lambda j, k: (0, 0))
    sel = pl.BlockSpec((n_heads, n_keys, tm), lambda j, k: (0, 0, j))
    once = dict(pipeline_mode=pl.Buffered(1))
    return pl.pallas_call(
        functools.partial(_peer_expert_kernel, n_heads=n_heads, n_keys=n_keys, alpha=alpha),
        grid=(n // tm, nk + 1),
        in_specs=[pl.BlockSpec((d, tm), lambda j, k: (0, j)),
                  pl.BlockSpec((te, d), lambda j, k: (jnp.minimum(k, nk - 1), 0)),
                  pl.BlockSpec((d, te), lambda j, k: (0, jnp.maximum(k - 1, 0))),
                  sel, sel, sel, sel,
                  pl.BlockSpec((tm, d), lambda j, k: (j, 0), **once),
                  vec, vec],
        out_specs=pl.BlockSpec((tm, d), lambda j, k: (j, 0), **once),
        out_shape=jax.ShapeDtypeStruct((n, d), _F32),
        scratch_shapes=[pltpu.VMEM((d, tm), _F32), pltpu.VMEM((te, tm), _BF16), pltpu.VMEM((te, tm), _BF16)],
        compiler_params=_params("parallel", "arbitrary"),
        name="peer_experts_ln2",
    )(x1t, u_b, vt_b, s2, t, e1, e2, x1, g.reshape(1, d), beta.reshape(1, d))


def _pick(n, pref):
    t = min(pref, n)
    while n % t:
        t //= 2
    return t


def kernel(x, w_in, w_out, b_out, dw_w, dw_b, conv_ln_g, conv_ln_b, rel_bias, ln1_g, ln1_b,
           peer_wq, peer_sub_keys, peer_u, peer_v, ln2_g, ln2_b):
    batch, seq, d = x.shape
    depth = w_in.shape[0]
    alpha = (2.0 * depth) ** 0.25
    d_conv = dw_w.shape[2]
    d_att = d - d_conv
    d_in = w_in.shape[2]
    n_idx_heads = (d_in - 2 * d_conv - 3 * d_att - IDX_DIM) // (IDX_DIM + 1)
    idx_w = n_idx_heads * IDX_DIM
    assert 2 * d_conv + 3 * d_att + idx_w + IDX_DIM + n_idx_heads == d_in
    n = batch * seq
    x2 = x.reshape(n, d)
    for l in range(depth):
        c0 = 2 * d_conv
        c1 = c0 + 3 * d_att + idx_w
        tail = jnp.pad(w_in[l][:, c1:], ((0, 0), (0, LANES - (d_in - c1))))
        w1 = jnp.concatenate([w_in[l][:, :c0], tail], axis=1).astype(_BF16)
        w2 = w_in[l][:, c0:c1].astype(_BF16)
        tm = _pick(n, 1024)
        proj1 = _matmul(x2, w1, _F32, tm, w1.shape[1])
        proj2 = _matmul(x2, w2, _BF16, tm, _pick(w2.shape[1], 1024))
        conv_out = _conv_group(proj1, dw_w[l], dw_b[l], conv_ln_g[l], conv_ln_b[l], batch, seq, d_conv,
                               _pick(seq, 256))
        att_out = _dsa_attention(proj2, proj1, rel_bias, batch, seq, d_att, d_conv, n_idx_heads)
        x1, x1b, x1t = _out_projection(conv_out, att_out, w_out[l].astype(_BF16), b_out[l], x2,
                                       ln1_g[l], ln1_b[l], alpha, _pick(n, 512))
        s2, t, e1, e2 = _peer_select(x1b, peer_wq[l].astype(_BF16), peer_sub_keys[l], _pick(n, 256))
        x2 = _peer_experts(x1, x1t, peer_u[l].astype(_BF16), peer_v[l].T.astype(_BF16), s2, t, e1, e2,
                           ln2_g[l], ln2_b[l], alpha, _pick(n, 512), 512)
    return x2.reshape(batch, seq, d)
```

```python
import functools
import math

import numpy as np
import jax
import jax.numpy as jnp
from jax import lax
from jax.experimental import pallas as pl
from jax.experimental.pallas import tpu as pltpu

HEAD_DIM = 64
IDX_DIM = 64
TOPK_MAX = 256
REL_MAX_DIST = 128
PEER_TOPK = 16
LN_EPS = 1e-5

LANES = 128
SUBLANES = 8
VMEM_LIMIT_BYTES = 56 * 1024 * 1024
NEG_BIG = -1e30
INT_MIN = -(2 ** 31)

_BF16 = jnp.bfloat16
_F32 = jnp.float32
_NT = (((1,), (1,)), ((), ()))
_TN = (((0,), (0,)), ((), ()))


def _params(*sem, flags=None):
    return pltpu.CompilerParams(dimension_semantics=sem, vmem_limit_bytes=VMEM_LIMIT_BYTES, flags=flags)


def _matmul_kernel(x_ref, w_ref, o_ref, xb_ref):
    @pl.when(pl.program_id(1) == 0)
    def _():
        xb_ref[...] = x_ref[...].astype(_BF16)

    o_ref[...] = jnp.dot(xb_ref[...], w_ref[...], preferred_element_type=_F32).astype(o_ref.dtype)


def _matmul(x, w, out_dtype, tm, tn):
    m, k = x.shape
    n = w.shape[1]
    assert m % tm == 0 and n % tn == 0
    return pl.pallas_call(
        _matmul_kernel,
        grid=(m // tm, n // tn),
        in_specs=[pl.BlockSpec((tm, k), lambda i, j: (i, 0)),
                  pl.BlockSpec((k, tn), lambda i, j: (0, j))],
        out_specs=pl.BlockSpec((tm, tn), lambda i, j: (i, j)),
        out_shape=jax.ShapeDtypeStruct((m, n), out_dtype),
        scratch_shapes=[pltpu.VMEM((tm, k), _BF16)],
        compiler_params=_params("parallel", "arbitrary"),
        name="proj_matmul",
    )(x, w)


CONV_HALO = 32
CONV_ROWS = 16


def _conv_kernel(a1_ref, a2_ref, h1_ref, h2_ref, w_ref, b_ref, g_ref, beta_ref, o_ref, hs_ref, rot_ref,
                 *, t_rows, width):
    first = pl.program_id(1) == 0
    halo = h1_ref[...] * jax.nn.sigmoid(h2_ref[...])
    hs_ref[0:CONV_HALO, :] = jnp.where(first, 0.0, halo)
    hs_ref[CONV_HALO:CONV_HALO + t_rows, :] = a1_ref[...] * jax.nn.sigmoid(a2_ref[...])
    n_rot = t_rows + CONV_HALO - SUBLANES
    rot_ref[0, :, :] = hs_ref[...]
    for r in range(1, SUBLANES):
        rot_ref[r, 0:n_rot, :] = hs_ref[r:r + n_rot, :]
    base = CONV_HALO - (width - 1)
    inv_c = 1.0 / o_ref.shape[-1]

    def chunk(c, carry):
        row0 = pl.multiple_of(c * CONV_ROWS, CONV_ROWS)
        acc = jnp.broadcast_to(b_ref[...], (CONV_ROWS, o_ref.shape[-1]))
        for j in range(width):
            q, r = divmod(base + j, SUBLANES)
            tap = rot_ref[r, pl.ds(row0 + q * SUBLANES, CONV_ROWS), :]
            acc = acc + w_ref[j:j + 1, :] * tap
        mu = jnp.sum(acc, axis=-1, keepdims=True) * inv_c
        d = acc - mu
        var = jnp.sum(d * d, axis=-1, keepdims=True) * inv_c
        y = d * lax.rsqrt(var + LN_EPS) * g_ref[...] + beta_ref[...]
        o_ref[pl.ds(row0, CONV_ROWS), :] = (y * jax.nn.sigmoid(y)).astype(o_ref.dtype)
        return carry

    lax.fori_loop(0, t_rows // CONV_ROWS, chunk, 0)


def _conv_group(proj1, dw_w, dw_b, cn_g, cn_b, batch, seq, d_conv, t_rows):
    width = dw_w.shape[0]
    assert width - 1 <= CONV_HALO and seq % t_rows == 0 and t_rows % CONV_HALO == 0
    nt = seq // t_rows
    hb = t_rows // CONV_HALO

    def cur(col):
        return pl.BlockSpec((t_rows, d_conv), lambda b, s: (b * nt + s, col))

    def halo(col):
        return pl.BlockSpec((CONV_HALO, d_conv), lambda b, s: (jnp.maximum((b * nt + s) * hb - 1, 0), col))

    vec = pl.BlockSpec((1, d_conv), lambda b, s: (0, 0))
    return pl.pallas_call(
        functools.partial(_conv_kernel, t_rows=t_rows, width=width),
        grid=(batch, nt),
        in_specs=[cur(0), cur(1), halo(0), halo(1),
                  pl.BlockSpec((width, d_conv), lambda b, s: (0, 0)), vec, vec, vec],
        out_specs=pl.BlockSpec((t_rows, d_conv), lambda b, s: (b * nt + s, 0)),
        out_shape=jax.ShapeDtypeStruct((batch * seq, d_conv), _BF16),
        scratch_shapes=[pltpu.VMEM((t_rows + CONV_HALO, d_conv), _F32),
                        pltpu.VMEM((SUBLANES, t_rows + CONV_HALO, d_conv), _F32)],
        compiler_params=_params("parallel", "arbitrary"),
        name="conv_group",
    )(proj1, proj1, proj1, proj1, dw_w, dw_b.reshape(1, -1), cn_g.reshape(1, -1), cn_b.reshape(1, -1))


ATT_TILE = 256
ATT_CHUNK = 4


def _sortable_key(x):
    b = pltpu.bitcast(x, jnp.int32)
    return b ^ ((b >> 31) & jnp.int32(0x7FFFFFFF))


def _fold8(x, op):
    r, c = x.shape
    x = x.reshape(r // SUBLANES, SUBLANES, c)
    return jnp.sum(x, axis=0) if op == "sum" else jnp.max(x, axis=0)


def _dsa_kernel(q_ref, k_ref, v_ref, qi_ref, kw_ref, kwq_ref, tb_ref, o_ref,
                kib_ref, keys_ref, maskb_ref, maskn_ref, lg_ref, lgn_ref, acc_ref, cut_ref,
                *, n_idx_heads, n_sel, n_pos):
    t = ATT_TILE
    i = pl.program_id(1)
    p = pl.program_id(2)
    near_base = jnp.maximum(i - 1, 0)
    n_far = near_base
    n_far_chunks = lax.shift_right_logical(n_far + (ATT_CHUNK - 1), ATT_CHUNK.bit_length() - 1)
    row = lax.broadcasted_iota(jnp.int32, (t, t), 0)
    col = lax.broadcasted_iota(jnp.int32, (t, t), 1)

    def causal(kt):
        return (kt * t + row) <= (i * t + col)

    @pl.when(p == 0)
    def _select():
        @pl.when(i == 0)
        def _():
            kib_ref[...] = kw_ref[:, 0:IDX_DIM].astype(_BF16)

        w_t = kwq_ref[...].T

        def score_tile(kt, carry):
            ki = kib_ref[pl.ds(pl.multiple_of(kt * t, t), t), :]
            acc = jnp.zeros((t, t), _F32)
            for h in range(n_idx_heads):
                z = lax.dot_general(ki, qi_ref[:, h * IDX_DIM:(h + 1) * IDX_DIM], _NT,
                                    preferred_element_type=_F32)
                acc = acc + jnp.maximum(z, 0.0) * w_t[IDX_DIM + h:IDX_DIM + h + 1, :]
            acc = jnp.where(causal(kt), acc, -jnp.inf)
            keys_ref[kt] = _sortable_key(acc)
            return carry

        lax.fori_loop(0, i + 1, score_tile, 0)

        def count_ge(cand):
            def body(kt, pc):
                return pc + _fold8((keys_ref[kt] >= cand).astype(jnp.int32), "sum")
            pc = lax.fori_loop(0, i + 1, body, jnp.zeros((SUBLANES, t), jnp.int32))
            return jnp.sum(pc, axis=0, keepdims=True)

        zero = jnp.zeros((1, t), jnp.int32)
        tau0 = jnp.where(count_ge(zero) >= n_sel, zero, jnp.int32(INT_MIN))

        def bit_step(it, tau):
            cand = tau + (jnp.int32(1) << (30 - it))
            return jnp.where(count_ge(cand) >= n_sel, cand, tau)

        tau = lax.fori_loop(0, 31, bit_step, tau0)

        n_gt = count_ge(tau + 1)
        need = n_sel - n_gt
        excess = count_ge(tau) - n_gt - need
        cut_ref[...] = jnp.full((1, t), n_pos, jnp.int32)

        @pl.when(jnp.max(excess) > 0)
        def _():
            def count_eq_below(bound):
                def body(kt, pc):
                    hit = (keys_ref[kt] == tau) & ((kt * t + row) < bound)
                    return pc + _fold8(hit.astype(jnp.int32), "sum")
                pc = lax.fori_loop(0, i + 1, body, jnp.zeros((SUBLANES, t), jnp.int32))
                return jnp.sum(pc, axis=0, keepdims=True)

            def pos_step(it, pos):
                cand = pos + (jnp.int32(n_pos) >> (it + 1))
                return jnp.where(count_eq_below(cand) < need, cand, pos)

            cut_ref[...] = lax.fori_loop(0, n_pos.bit_length() - 1, pos_step, jnp.zeros((1, t), jnp.int32))

        cut = cut_ref[...]

        def mask_tile(kt, carry):
            key = keys_ref[kt]
            sel = ((key > tau) | ((key == tau) & ((kt * t + row) <= cut))) & causal(kt)
            maskb_ref[kt] = jnp.where(sel, 0.0, NEG_BIG)
            return carry

        lax.fori_loop(0, i + 1, mask_tile, 0)
        neg_tile = jnp.full((t, t), NEG_BIG, _F32)
        maskn_ref[0] = maskb_ref[near_base]
        maskn_ref[1] = jnp.where(i > 0, maskb_ref[i], neg_tile)

        def pad_tile(kt, carry):
            maskb_ref[kt] = neg_tile
            return carry

        lax.fori_loop(n_far, n_far_chunks * ATT_CHUNK, pad_tile, 0)

    ck = ATT_CHUNK * t
    lane = lax.broadcasted_iota(jnp.int32, (t, LANES), 1)
    q_all = q_ref[...] * jnp.asarray(HEAD_DIM ** -0.5, _BF16)
    qm = [jnp.where((lane >= hl * HEAD_DIM) & (lane < (hl + 1) * HEAD_DIM), q_all, jnp.zeros_like(q_all))
          for hl in range(2)]
    near_rows = pl.ds(pl.multiple_of(near_base * t, t), 2 * t)
    near_bias = (jnp.where(i > 0, 0, 1), 1)

    def far_logits(c, mps):
        kk = k_ref[pl.ds(pl.multiple_of(c * ck, ck), ck), :]
        new = []
        for hl in range(2):
            st = lax.dot_general(kk, qm[hl], _NT, preferred_element_type=_F32)
            mp = mps[hl]
            for u in range(ATT_CHUNK):
                tile = st[u * t:(u + 1) * t, :] + maskb_ref[c * ATT_CHUNK + u]
                lg_ref[hl, c * ATT_CHUNK + u] = tile
                mp = jnp.maximum(mp, _fold8(tile, "max"))
            new.append(mp)
        return tuple(new)

    mp0 = jnp.full((SUBLANES, t), NEG_BIG, _F32)
    mps = lax.fori_loop(0, n_far_chunks, far_logits, (mp0, mp0))
    kk = k_ref[near_rows, :]
    m = []
    for hl in range(2):
        st = lax.dot_general(kk, qm[hl], _NT, preferred_element_type=_F32)
        mp = mps[hl]
        for s in range(2):
            tile = st[s * t:(s + 1) * t, :] + maskn_ref[s] + tb_ref[hl, near_bias[s]]
            lgn_ref[hl, s] = tile
            mp = jnp.maximum(mp, _fold8(tile, "max"))
        m.append(jnp.max(mp, axis=0, keepdims=True))

    acc_ref[...] = jnp.zeros_like(acc_ref)

    def far_pv(c, lps):
        vv = v_ref[pl.ds(pl.multiple_of(c * ck, ck), ck), :]
        new = []
        for hl in range(2):
            pr = jnp.exp(lg_ref[hl, pl.ds(c * ATT_CHUNK, ATT_CHUNK)].reshape(ck, t) - m[hl])
            new.append(lps[hl] + _fold8(pr, "sum"))
            acc_ref[hl] += lax.dot_general(vv, pr.astype(_BF16), _TN, preferred_element_type=_F32)
        return tuple(new)

    lp0 = jnp.zeros((SUBLANES, t), _F32)
    lps = lax.fori_loop(0, n_far_chunks, far_pv, (lp0, lp0))
    vv = v_ref[near_rows, :]
    outs = []
    for hl in range(2):
        pr = jnp.exp(lgn_ref[hl].reshape(2 * t, t) - m[hl])
        lp = lps[hl] + _fold8(pr, "sum")
        o_t = acc_ref[hl] + lax.dot_general(vv, pr.astype(_BF16), _TN, preferred_element_type=_F32)
        o_t = o_t / jnp.sum(lp, axis=0, keepdims=True)
        outs.append(o_t[hl * HEAD_DIM:(hl + 1) * HEAD_DIM, :])
    o_ref[...] = jnp.concatenate(outs, axis=0).T.astype(o_ref.dtype)


def _rel_bucket_table(n_buckets):
    d = np.arange(REL_MAX_DIST, dtype=np.int32)
    max_exact = n_buckets // 2
    ratio = np.log(np.maximum(d, 1).astype(np.float32) / np.float32(max_exact)) / np.float32(
        math.log(REL_MAX_DIST / max_exact)) * np.float32(n_buckets - max_exact)
    large = np.minimum(max_exact + ratio.astype(np.int32), n_buckets - 1)
    return np.where(d < max_exact, d, large)


def _band_bias(rel_bias):
    n_buckets, n_heads = rel_bias.shape
    t = ATT_TILE
    assert REL_MAX_DIST <= t
    rb = rel_bias.astype(_F32)
    shifted = rb - rb[n_buckets - 1][None, :]
    by_dist = shifted[_rel_bucket_table(n_buckets)].T
    dv = jnp.concatenate([jnp.zeros((n_heads, t - 1), _F32), by_dist,
                          jnp.zeros((n_heads, 2 * t - REL_MAX_DIST + 1), _F32)], axis=1)
    slots = []
    for off in (t, 0):
        v = dv[:, off:off + 2 * t]
        flat = jnp.tile(v, (1, t))[:, t - 1:t - 1 + t * (2 * t - 1)]
        slots.append(flat.reshape(n_heads, t, 2 * t - 1)[:, :, :t])
    return jnp.stack(slots, axis=1)


def _dsa_attention(proj2, proj1, rel_bias, batch, seq, d_att, d_conv, n_idx_heads):
    t = ATT_TILE
    n_heads = d_att // HEAD_DIM
    n_pairs = n_heads // 2
    idx_w = n_idx_heads * IDX_DIM
    assert seq % t == 0 and n_heads % 2 == 0 and (3 * d_att) % idx_w == 0 and (2 * d_conv) % LANES == 0
    assert IDX_DIM + n_idx_heads <= LANES
    nq = seq // t
    n_sel = min(TOPK_MAX, seq // 4)
    assert n_sel <= t and seq & (seq - 1) == 0 and nq >= 2 and nq % ATT_CHUNK == 0
    cb = d_att // LANES
    kwb = 2 * d_conv // LANES
    tb = _band_bias(rel_bias)
    return pl.pallas_call(
        functools.partial(_dsa_kernel, n_idx_heads=n_idx_heads, n_sel=n_sel, n_pos=seq),
        grid=(batch, nq, n_pairs),
        in_specs=[pl.BlockSpec((t, LANES), lambda b, i, p: (b * nq + i, p)),
                  pl.BlockSpec((seq, LANES), lambda b, i, p: (b, cb + p)),
                  pl.BlockSpec((seq, LANES), lambda b, i, p: (b, 2 * cb + p)),
                  pl.BlockSpec((t, idx_w), lambda b, i, p: (b * nq + i, 3 * d_att // idx_w)),
                  pl.BlockSpec((seq, LANES), lambda b, i, p: (b, kwb)),
                  pl.BlockSpec((t, LANES), lambda b, i, p: (b * nq + i, kwb)),
                  pl.BlockSpec((2, 2, t, t), lambda b, i, p: (p, 0, 0, 0))],
        out_specs=pl.BlockSpec((t, LANES), lambda b, i, p: (b * nq + i, p)),
        out_shape=jax.ShapeDtypeStruct((batch * seq, d_att), _BF16),
        scratch_shapes=[pltpu.VMEM((seq, IDX_DIM), _BF16),
                        pltpu.VMEM((nq, t, t), jnp.int32),
                        pltpu.VMEM((nq, t, t), _F32),
                        pltpu.VMEM((2, t, t), _F32),
                        pltpu.VMEM((2, nq, t, t), _F32),
                        pltpu.VMEM((2, 2, t, t), _F32),
                        pltpu.VMEM((2, LANES, t), _F32),
                        pltpu.VMEM((1, t), jnp.int32)],
        compiler_params=_params("parallel", "arbitrary", "arbitrary"),
        name="dsa_attention",
    )(proj2, proj2, proj2, proj2, proj1, proj1, tb)


def _layer_norm(y, g, b):
    mu = jnp.mean(y, axis=-1, keepdims=True)
    d = y - mu
    var = jnp.mean(d * d, axis=-1, keepdims=True)
    return d * lax.rsqrt(var + LN_EPS) * g + b


def _outproj_kernel(c_ref, a_ref, w1_ref, w2_ref, b_ref, x_ref, g_ref, beta_ref, o_ref, ob_ref, ot_ref, *, alpha):
    mix = jnp.dot(c_ref[...], w1_ref[...], preferred_element_type=_F32)
    mix = mix + jnp.dot(a_ref[...], w2_ref[...], preferred_element_type=_F32) + b_ref[...]
    y = _layer_norm(alpha * x_ref[...] + mix, g_ref[...], beta_ref[...])
    o_ref[...] = y
    ob_ref[...] = y.astype(_BF16)
    ot_ref[...] = y.T.astype(_BF16)


def _out_projection(conv_out, att_out, w_out, b_out, x2, g, beta, alpha, tm):
    n, d = x2.shape
    dc = conv_out.shape[1]
    da = att_out.shape[1]
    vec = pl.BlockSpec((1, d), lambda i: (0, 0))
    return pl.pallas_call(
        functools.partial(_outproj_kernel, alpha=alpha),
        grid=(n // tm,),
        in_specs=[pl.BlockSpec((tm, dc), lambda i: (i, 0)),
                  pl.BlockSpec((tm, da), lambda i: (i, 0)),
                  pl.BlockSpec((dc, d), lambda i: (0, 0)),
                  pl.BlockSpec((da, d), lambda i: (0, 0)),
                  vec,
                  pl.BlockSpec((tm, d), lambda i: (i, 0)),
                  vec, vec],
        out_specs=[pl.BlockSpec((tm, d), lambda i: (i, 0)), pl.BlockSpec((tm, d), lambda i: (i, 0)),
                   pl.BlockSpec((d, tm), lambda i: (0, i))],
        out_shape=[jax.ShapeDtypeStruct((n, d), _F32), jax.ShapeDtypeStruct((n, d), _BF16),
                   jax.ShapeDtypeStruct((d, n), _BF16)],
        compiler_params=_params("parallel"),
        name="out_projection_ln1",
    )(conv_out, att_out, w_out[:dc], w_out[dc:], b_out.reshape(1, d), x2, g.reshape(1, d), beta.reshape(1, d))


def _peer_pairs(k):
    return [(i, j) for i in range(k) for j in range(k) if (i + 1) * (j + 1) <= k]


PEER_POPS = PEER_TOPK + 1


def _batcher_pairs(n):
    pairs = []
    p = 1
    while p < n:
        k = p
        while k >= 1:
            for j in range(k % p, n - k, 2 * k):
                for i in range(min(k, n - j - k)):
                    if (i + j) // (2 * p) == (i + j + k) // (2 * p):
                        pairs.append((i + j, i + j + k))
            k //= 2
        p *= 2
    return pairs


def _sort_desc(xs):
    xs = list(xs)
    for i, j in _batcher_pairs(len(xs)):
        xs[i], xs[j] = jnp.maximum(xs[i], xs[j]), jnp.minimum(xs[i], xs[j])
    return xs


def _bitonic_merge_desc(xs):
    xs = list(xs)
    n = len(xs)
    k = n // 2
    while k >= 1:
        for i in range(n):
            if i & k == 0:
                xs[i], xs[i + k] = jnp.maximum(xs[i], xs[i + k]), jnp.minimum(xs[i], xs[i + k])
        k //= 2
    return xs


def _partner(xs, shift):
    return [pltpu.roll(x, shift, 0) for x in xs]


def _merge_keep(a, b, dropped):
    n = len(a)
    hi = [jnp.maximum(a[i], b[n - 1 - i]) for i in range(n)]
    for i in range(n):
        dropped = jnp.maximum(dropped, jnp.minimum(a[i], b[n - 1 - i]))
    return _bitonic_merge_desc(hi), dropped


def _top_across_sublanes(xs, keep):
    dropped = jnp.full(xs[0].shape, -jnp.inf, _F32)
    for shift in (4, 2, 1):
        other = _partner(xs, shift)
        if 2 * len(xs) <= keep:
            xs = _bitonic_merge_desc(xs + other[::-1])
        else:
            assert len(xs) == keep
            xs, dropped = _merge_keep(xs, other, jnp.maximum(dropped, pltpu.roll(dropped, shift, 0)))
    return xs, dropped


def _pack_sublanes(slabs, sub):
    out = []
    for g in range(-(-len(slabs) // SUBLANES)):
        acc = jnp.full(slabs[0].shape, -jnp.inf, _F32)
        for s in range(SUBLANES):
            if g * SUBLANES + s < len(slabs):
                acc = jnp.where(sub == s, slabs[g * SUBLANES + s], acc)
        out.append(acc)
    return out


def _peer_select_kernel(x_ref, wq_ref, sk_ref, s2_ref, t_ref, e1_ref, e2_ref, *, n_heads, n_keys):
    tm = x_ref.shape[0]
    dk = sk_ref.shape[2]
    q = jnp.dot(x_ref[...], wq_ref[...], preferred_element_type=_F32).astype(_BF16)
    sub = lax.broadcasted_iota(jnp.int32, (SUBLANES, tm), 0)
    pairs = _peer_pairs(PEER_POPS)
    for h in range(n_heads):
        tops = []
        scores = []
        for c in range(2):
            g = 2 * h + c
            s_t = lax.dot_general(sk_ref[g], q[:, g * dk:(g + 1) * dk], _NT,
                                  preferred_element_type=_F32)
            scores.append(s_t)
            groups = _sort_desc([s_t[r * SUBLANES:(r + 1) * SUBLANES, :] for r in range(n_keys // SUBLANES)])
            top, nxt = _top_across_sublanes(groups, PEER_TOPK)
            tops.append(top + [nxt])
        a, b = tops
        cand = _pack_sublanes([a[i] + b[j] for (i, j) in pairs], sub)
        cand += [jnp.full((SUBLANES, tm), -jnp.inf, _F32)] * (SUBLANES - len(cand))
        top, nxt = _top_across_sublanes(_sort_desc(cand), PEER_TOPK)
        vals = [v[0:1, :] for v in top + [nxt]]
        a = [v[0:1, :] for v in a]
        b = [v[0:1, :] for v in b]
        z = jnp.zeros((1, tm), _F32)
        for v in vals[:PEER_TOPK]:
            z = z + jnp.exp(v - vals[0])
        theta = 0.5 * (vals[PEER_TOPK - 1] + vals[PEER_TOPK])
        s2_ref[h] = scores[1]
        t_ref[h] = theta - scores[0]
        e1_ref[h] = jnp.exp(scores[0] - a[0])
        e2_ref[h] = jnp.exp(scores[1] - b[0]) / z


def _peer_select(x1b, wq, sub_keys, tm):
    n, d = x1b.shape
    n_heads, _, n_keys, dk = sub_keys.shape
    assert n_keys == PEER_TOPK * SUBLANES
    sk = sub_keys.reshape(2 * n_heads, n_keys, dk).astype(_BF16)
    g2 = 2 * n_heads
    out = pl.BlockSpec((n_heads, n_keys, tm), lambda i: (0, 0, i))
    shape = jax.ShapeDtypeStruct((n_heads, n_keys, n), _F32)
    return pl.pallas_call(
        functools.partial(_peer_select_kernel, n_heads=n_heads, n_keys=n_keys),
        grid=(n // tm,),
        in_specs=[pl.BlockSpec((tm, d), lambda i: (i, 0)),
                  pl.BlockSpec((d, g2 * dk), lambda i: (0, 0)),
                  pl.BlockSpec((g2, n_keys, dk), lambda i: (0, 0, 0))],
        out_specs=[out, out, out, out],
        out_shape=[shape, shape, shape, shape],
        compiler_params=_params("parallel"),
        name="peer_select",
    )(x1b, wq, sk)


PEER_SUB = 256


def _peer_expert_kernel(xt_ref, u_ref, vt_ref, s2_ref, t_ref, e1_ref, e2_ref, x_ref, g_ref, beta_ref, o_ref,
                        y_ref, a0_ref, a1_ref, *, n_heads, n_keys, alpha):
    k = pl.program_id(1)
    nk = pl.num_programs(1) - 1
    te = u_ref.shape[0]
    tm = xt_ref.shape[1]
    d = vt_ref.shape[0]
    kt = jnp.minimum(k, nk - 1)
    units = [(r, c) for r in range(te // n_keys) for c in range(tm // PEER_SUB)]
    n_pieces = d // PEER_SUB
    pieces_of = [[m for m in range(n_pieces) if m * len(units) // n_pieces == ui] for ui in range(len(units))]

    @pl.when(k == 0)
    def _():
        y_ref[...] = jnp.zeros_like(y_ref)
        a1_ref[...] = jnp.zeros_like(a1_ref)

    def body(a_cur, a_prev):
        def second_matmul_piece(m):
            rows = slice(m * PEER_SUB, (m + 1) * PEER_SUB)
            y_ref[rows, :] += jnp.dot(vt_ref[rows, :], a_prev[...], preferred_element_type=_F32)

        for ui, (r, c2) in enumerate(units):
            rows = slice(r * n_keys, (r + 1) * n_keys)
            h_t = jnp.dot(u_ref[rows, :], xt_ref[:, c2 * PEER_SUB:(c2 + 1) * PEER_SUB],
                          preferred_element_type=_F32)
            for m in pieces_of[ui]:
                second_matmul_piece(m)
            act = 0.5 * h_t * (1.0 + lax.erf(h_t * (2.0 ** -0.5)))
            i1 = kt * (te // n_keys) + r
            for c in range(PEER_SUB // LANES):
                cols = slice(c2 * PEER_SUB + c * LANES, c2 * PEER_SUB + (c + 1) * LANES)
                gate = jnp.zeros((n_keys, LANES), _F32)
                for h in range(n_heads):
                    sel = s2_ref[h, :, cols] >= t_ref[h, pl.ds(i1, 1), :][:, cols]
                    gate = gate + jnp.where(sel, e2_ref[h, :, cols] * e1_ref[h, pl.ds(i1, 1), :][:, cols], 0.0)
                a_cur[rows, cols] = (gate * act[:, c * LANES:(c + 1) * LANES]).astype(a_cur.dtype)

    @pl.when(k % 2 == 0)
    def _():
        body(a0_ref, a1_ref)

    @pl.when(k % 2 == 1)
    def _():
        body(a1_ref, a0_ref)

    @pl.when(k == nk)
    def _():
        y = alpha * x_ref[...] + y_ref[...].T
        o_ref[...] = _layer_norm(y, g_ref[...], beta_ref[...])


def _peer_experts(x1, x1t, u_b, vt_b, s2, t, e1, e2, g, beta, alpha, tm, te):
    n, d = x1.shape
    n_exp = u_b.shape[0]
    n_heads, n_keys, _ = s2.shape
    assert n_exp == n_keys * n_keys and te % n_keys == 0 and d % PEER_SUB == 0
    assert n_exp % te == 0 and n % tm == 0 and tm % PEER_SUB == 0
    nk = n_exp // te
    vec = pl.BlockSpec((1, d), lambda j, k: (0, 0))
    sel = pl.BlockSpec((n_heads, n_keys, tm), lambda j, k: (0, 0, j))
    once = dict(pipeline_mode=pl.Buffered(1))
    return pl.pallas_call(
        functools.partial(_peer_expert_kernel, n_heads=n_heads, n_keys=n_keys, alpha=alpha),
        grid=(n // tm, nk + 1),
        in_specs=[pl.BlockSpec((d, tm), lambda j, k: (0, j)),
                  pl.BlockSpec((te, d), lambda j, k: (jnp.minimum(k, nk - 1), 0)),
                  pl.BlockSpec((d, te), lambda j, k: (0, jnp.maximum(k - 1, 0))),
                  sel, sel, sel, sel,
                  pl.BlockSpec((tm, d), lambda j, k: (j, 0), **once),
                  vec, vec],
        out_specs=pl.BlockSpec((tm, d), lambda j, k: (j, 0), **once),
        out_shape=jax.ShapeDtypeStruct((n, d), _F32),
        scratch_shapes=[pltpu.VMEM((d, tm), _F32), pltpu.VMEM((te, tm), _BF16), pltpu.VMEM((te, tm), _BF16)],
        compiler_params=_params("parallel", "arbitrary"),
        name="peer_experts_ln2",
    )(x1t, u_b, vt_b, s2, t, e1, e2, x1, g.reshape(1, d), beta.reshape(1, d))


def _pick(n, pref):
    t = min(pref, n)
    while n % t:
        t //= 2
    return t


def kernel(x, w_in, w_out, b_out, dw_w, dw_b, conv_ln_g, conv_ln_b, rel_bias, ln1_g, ln1_b,
           peer_wq, peer_sub_keys, peer_u, peer_v, ln2_g, ln2_b):
    batch, seq, d = x.shape
    depth = w_in.shape[0]
    alpha = (2.0 * depth) ** 0.25
    d_conv = dw_w.shape[2]
    d_att = d - d_conv
    d_in = w_in.shape[2]
    n_idx_heads = (d_in - 2 * d_conv - 3 * d_att - IDX_DIM) // (IDX_DIM + 1)
    idx_w = n_idx_heads * IDX_DIM
    assert 2 * d_conv + 3 * d_att + idx_w + IDX_DIM + n_idx_heads == d_in
    n = batch * seq
    x2 = x.reshape(n, d)
    for l in range(depth):
        c0 = 2 * d_conv
        c1 = c0 + 3 * d_att + idx_w
        tail = jnp.pad(w_in[l][:, c1:], ((0, 0), (0, LANES - (d_in - c1))))
        w1 = jnp.concatenate([w_in[l][:, :c0], tail], axis=1).astype(_BF16)
        w2 = w_in[l][:, c0:c1].astype(_BF16)
        tm = _pick(n, 1024)
        proj1 = _matmul(x2, w1, _F32, tm, w1.shape[1])
        proj2 = _matmul(x2, w2, _BF16, tm, _pick(w2.shape[1], 1024))
        conv_out = _conv_group(proj1, dw_w[l], dw_b[l], conv_ln_g[l], conv_ln_b[l], batch, seq, d_conv,
                               _pick(seq, 256))
        att_out = _dsa_attention(proj2, proj1, rel_bias, batch, seq, d_att, d_conv, n_idx_heads)
        x1, x1b, x1t = _out_projection(conv_out, att_out, w_out[l].astype(_BF16), b_out[l], x2,
                                       ln1_g[l], ln1_b[l], alpha, _pick(n, 512))
        s2, t, e1, e2 = _peer_select(x1b, peer_wq[l].astype(_BF16), peer_sub_keys[l], _pick(n, 256))
        x2 = _peer_experts(x1, x1t, peer_u[l].astype(_BF16), peer_v[l].T.astype(_BF16), s2, t, e1, e2,
                           ln2_g[l], ln2_b[l], alpha, _pick(n, 512), 512)
    return x2.reshape(batch, seq, d)
```

```python
import functools
import math

import numpy as np
import jax
import jax.numpy as jnp
from jax import lax
from jax.experimental import pallas as pl
from jax.experimental.pallas import tpu as pltpu

HEAD_DIM = 64
IDX_DIM = 64
TOPK_MAX = 256
REL_MAX_DIST = 128
PEER_TOPK = 16
LN_EPS = 1e-5

LANES = 128
SUBLANES = 8
VMEM_LIMIT_BYTES = 56 * 1024 * 1024
NEG_BIG = -1e30
INT_MIN = -(2 ** 31)

_BF16 = jnp.bfloat16
_F32 = jnp.float32
_NT = (((1,), (1,)), ((), ()))
_TN = (((0,), (0,)), ((), ()))


def _params(*sem, flags=None):
    return pltpu.CompilerParams(dimension_semantics=sem, vmem_limit_bytes=VMEM_LIMIT_BYTES, flags=flags)


def _matmul_kernel(x_ref, w_ref, o_ref, xb_ref):
    @pl.when(pl.program_id(1) == 0)
    def _():
        xb_ref[...] = x_ref[...].astype(_BF16)

    o_ref[...] = jnp.dot(xb_ref[...], w_ref[...], preferred_element_type=_F32).astype(o_ref.dtype)


def _matmul(x, w, out_dtype, tm, tn):
    m, k = x.shape
    n = w.shape[1]
    assert m % tm == 0 and n % tn == 0
    return pl.pallas_call(
        _matmul_kernel,
        grid=(m // tm, n // tn),
        in_specs=[pl.BlockSpec((tm, k), lambda i, j: (i, 0)),
                  pl.BlockSpec((k, tn), lambda i, j: (0, j))],
        out_specs=pl.BlockSpec((tm, tn), lambda i, j: (i, j)),
        out_shape=jax.ShapeDtypeStruct((m, n), out_dtype),
        scratch_shapes=[pltpu.VMEM((tm, k), _BF16)],
        compiler_params=_params("parallel", "arbitrary"),
        name="proj_matmul",
    )(x, w)


CONV_HALO = 32
CONV_ROWS = 16
CONV_UNROLL = 4


def _conv_kernel(a1_ref, a2_ref, h1_ref, h2_ref, w_ref, b_ref, g_ref, beta_ref, o_ref, hs_ref, rot_ref,
                 *, t_rows, width):
    first = pl.program_id(1) == 0
    halo = h1_ref[...] * jax.nn.sigmoid(h2_ref[...])
    hs_ref[0:CONV_HALO, :] = jnp.where(first, 0.0, halo)
    hs_ref[CONV_HALO:CONV_HALO + t_rows, :] = a1_ref[...] * jax.nn.sigmoid(a2_ref[...])
    n_rot = t_rows + CONV_HALO - SUBLANES
    rot_ref[0, :, :] = hs_ref[...]
    for r in range(1, SUBLANES):
        rot_ref[r, 0:n_rot, :] = hs_ref[r:r + n_rot, :]
    base = CONV_HALO - (width - 1)
    inv_c = 1.0 / o_ref.shape[-1]

    def chunk(c, carry):
        row0 = pl.multiple_of(c * CONV_ROWS, CONV_ROWS)
        acc = jnp.broadcast_to(b_ref[...], (CONV_ROWS, o_ref.shape[-1]))
        for j in range(width):
            q, r = divmod(base + j, SUBLANES)
            tap = rot_ref[r, pl.ds(row0 + q * SUBLANES, CONV_ROWS), :]
            acc = acc + w_ref[j:j + 1, :] * tap
        mu = jnp.sum(acc, axis=-1, keepdims=True) * inv_c
        d = acc - mu
        var = jnp.sum(d * d, axis=-1, keepdims=True) * inv_c
        y = d * lax.rsqrt(var + LN_EPS) * g_ref[...] + beta_ref[...]
        o_ref[pl.ds(row0, CONV_ROWS), :] = (y * jax.nn.sigmoid(y)).astype(o_ref.dtype)
        return carry

    lax.fori_loop(0, t_rows // CONV_ROWS, chunk, 0, unroll=CONV_UNROLL)


def _conv_group(proj1, dw_w, dw_b, cn_g, cn_b, batch, seq, d_conv, t_rows):
    width = dw_w.shape[0]
    assert width - 1 <= CONV_HALO and seq % t_rows == 0 and t_rows % CONV_HALO == 0
    nt = seq // t_rows
    hb = t_rows // CONV_HALO

    def cur(col):
        return pl.BlockSpec((t_rows, d_conv), lambda b, s: (b * nt + s, col))

    def halo(col):
        return pl.BlockSpec((CONV_HALO, d_conv), lambda b, s: (jnp.maximum((b * nt + s) * hb - 1, 0), col))

    vec = pl.BlockSpec((1, d_conv), lambda b, s: (0, 0))
    return pl.pallas_call(
        functools.partial(_conv_kernel, t_rows=t_rows, width=width),
        grid=(batch, nt),
        in_specs=[cur(0), cur(1), halo(0), halo(1),
                  pl.BlockSpec((width, d_conv), lambda b, s: (0, 0)), vec, vec, vec],
        out_specs=pl.BlockSpec((t_rows, d_conv), lambda b, s: (b * nt + s, 0)),
        out_shape=jax.ShapeDtypeStruct((batch * seq, d_conv), _BF16),
        scratch_shapes=[pltpu.VMEM((t_rows + CONV_HALO, d_conv), _F32),
                        pltpu.VMEM((SUBLANES, t_rows + CONV_HALO, d_conv), _F32)],
        compiler_params=_params("parallel", "arbitrary"),
        name="conv_group",
    )(proj1, proj1, proj1, proj1, dw_w, dw_b.reshape(1, -1), cn_g.reshape(1, -1), cn_b.reshape(1, -1))


ATT_TILE = 256
ATT_CHUNK = 4


def _sortable_key(x):
    b = pltpu.bitcast(x, jnp.int32)
    return b ^ ((b >> 31) & jnp.int32(0x7FFFFFFF))


def _fold8(x, op):
    r, c = x.shape
    x = x.reshape(r // SUBLANES, SUBLANES, c)
    return jnp.sum(x, axis=0) if op == "sum" else jnp.max(x, axis=0)


def _dsa_kernel(q_ref, k_ref, v_ref, qi_ref, kw_ref, kwq_ref, tb_ref, o_ref,
                kib_ref, keys_ref, maskb_ref, maskn_ref, lg_ref, lgn_ref, acc_ref, cut_ref,
                *, n_idx_heads, n_sel, n_pos):
    t = ATT_TILE
    i = pl.program_id(1)
    p = pl.program_id(2)
    near_base = jnp.maximum(i - 1, 0)
    n_far = near_base
    n_far_chunks = lax.shift_right_logical(n_far + (ATT_CHUNK - 1), ATT_CHUNK.bit_length() - 1)
    row = lax.broadcasted_iota(jnp.int32, (t, t), 0)
    col = lax.broadcasted_iota(jnp.int32, (t, t), 1)

    def causal(kt):
        return (kt * t + row) <= (i * t + col)

    @pl.when(p == 0)
    def _select():
        @pl.when(i == 0)
        def _():
            kib_ref[...] = kw_ref[:, 0:IDX_DIM].astype(_BF16)

        w_t = kwq_ref[...].T

        def score_tile(kt, carry):
            ki = kib_ref[pl.ds(pl.multiple_of(kt * t, t), t), :]
            acc = jnp.zeros((t, t), _F32)
            for h in range(n_idx_heads):
                z = lax.dot_general(ki, qi_ref[:, h * IDX_DIM:(h + 1) * IDX_DIM], _NT,
                                    preferred_element_type=_F32)
                acc = acc + jnp.maximum(z, 0.0) * w_t[IDX_DIM + h:IDX_DIM + h + 1, :]
            acc = jnp.where(causal(kt), acc, -jnp.inf)
            keys_ref[kt] = _sortable_key(acc)
            return carry

        lax.fori_loop(0, i + 1, score_tile, 0)

        def count_ge(cand):
            def body(kt, pc):
                return pc + _fold8((keys_ref[kt] >= cand).astype(jnp.int32), "sum")
            pc = lax.fori_loop(0, i + 1, body, jnp.zeros((SUBLANES, t), jnp.int32))
            return jnp.sum(pc, axis=0, keepdims=True)

        zero = jnp.zeros((1, t), jnp.int32)
        tau0 = jnp.where(count_ge(zero) >= n_sel, zero, jnp.int32(INT_MIN))

        def bit_step(it, tau):
            cand = tau + (jnp.int32(1) << (30 - it))
            return jnp.where(count_ge(cand) >= n_sel, cand, tau)

        tau = lax.fori_loop(0, 31, bit_step, tau0)

        n_gt = count_ge(tau + 1)
        need = n_sel - n_gt
        excess = count_ge(tau) - n_gt - need
        cut_ref[...] = jnp.full((1, t), n_pos, jnp.int32)

        @pl.when(jnp.max(excess) > 0)
        def _():
            def count_eq_below(bound):
                def body(kt, pc):
                    hit = (keys_ref[kt] == tau) & ((kt * t + row) < bound)
                    return pc + _fold8(hit.astype(jnp.int32), "sum")
                pc = lax.fori_loop(0, i + 1, body, jnp.zeros((SUBLANES, t), jnp.int32))
                return jnp.sum(pc, axis=0, keepdims=True)

            def pos_step(it, pos):
                cand = pos + (jnp.int32(n_pos) >> (it + 1))
                return jnp.where(count_eq_below(cand) < need, cand, pos)

            cut_ref[...] = lax.fori_loop(0, n_pos.bit_length() - 1, pos_step, jnp.zeros((1, t), jnp.int32))

        cut = cut_ref[...]

        def mask_tile(kt, carry):
            key = keys_ref[kt]
            sel = ((key > tau) | ((key == tau) & ((kt * t + row) <= cut))) & causal(kt)
            maskb_ref[kt] = jnp.where(sel, 0.0, NEG_BIG)
            return carry

        lax.fori_loop(0, i + 1, mask_tile, 0)
        neg_tile = jnp.full((t, t), NEG_BIG, _F32)
        maskn_ref[0] = maskb_ref[near_base]
        maskn_ref[1] = jnp.where(i > 0, maskb_ref[i], neg_tile)

        def pad_tile(kt, carry):
            maskb_ref[kt] = neg_tile
            return carry

        lax.fori_loop(n_far, n_far_chunks * ATT_CHUNK, pad_tile, 0)

    ck = ATT_CHUNK * t
    lane = lax.broadcasted_iota(jnp.int32, (t, LANES), 1)
    q_all = q_ref[...] * jnp.asarray(HEAD_DIM ** -0.5, _BF16)
    qm = [jnp.where((lane >= hl * HEAD_DIM) & (lane < (hl + 1) * HEAD_DIM), q_all, jnp.zeros_like(q_all))
          for hl in range(2)]
    near_rows = pl.ds(pl.multiple_of(near_base * t, t), 2 * t)
    near_bias = (jnp.where(i > 0, 0, 1), 1)

    def far_logits(c, mps):
        kk = k_ref[pl.ds(pl.multiple_of(c * ck, ck), ck), :]
        new = []
        for hl in range(2):
            st = lax.dot_general(kk, qm[hl], _NT, preferred_element_type=_F32)
            mp = mps[hl]
            for u in range(ATT_CHUNK):
                tile = st[u * t:(u + 1) * t, :] + maskb_ref[c * ATT_CHUNK + u]
                lg_ref[hl, c * ATT_CHUNK + u] = tile
                mp = jnp.maximum(mp, _fold8(tile, "max"))
            new.append(mp)
        return tuple(new)

    mp0 = jnp.full((SUBLANES, t), NEG_BIG, _F32)
    mps = lax.fori_loop(0, n_far_chunks, far_logits, (mp0, mp0))
    kk = k_ref[near_rows, :]
    m = []
    for hl in range(2):
        st = lax.dot_general(kk, qm[hl], _NT, preferred_element_type=_F32)
        mp = mps[hl]
        for s in range(2):
            tile = st[s * t:(s + 1) * t, :] + maskn_ref[s] + tb_ref[hl, near_bias[s]]
            lgn_ref[hl, s] = tile
            mp = jnp.maximum(mp, _fold8(tile, "max"))
        m.append(jnp.max(mp, axis=0, keepdims=True))

    acc_ref[...] = jnp.zeros_like(acc_ref)

    def far_pv(c, lps):
        vv = v_ref[pl.ds(pl.multiple_of(c * ck, ck), ck), :]
        new = []
        for hl in range(2):
            pr = jnp.exp(lg_ref[hl, pl.ds(c * ATT_CHUNK, ATT_CHUNK)].reshape(ck, t) - m[hl])
            new.append(lps[hl] + _fold8(pr, "sum"))
            acc_ref[hl] += lax.dot_general(vv, pr.astype(_BF16), _TN, preferred_element_type=_F32)
        return tuple(new)

    lp0 = jnp.zeros((SUBLANES, t), _F32)
    lps = lax.fori_loop(0, n_far_chunks, far_pv, (lp0, lp0))
    vv = v_ref[near_rows, :]
    outs = []
    for hl in range(2):
        pr = jnp.exp(lgn_ref[hl].reshape(2 * t, t) - m[hl])
        lp = lps[hl] + _fold8(pr, "sum")
        o_t = acc_ref[hl] + lax.dot_general(vv, pr.astype(_BF16), _TN, preferred_element_type=_F32)
        o_t = o_t / jnp.sum(lp, axis=0, keepdims=True)
        outs.append(o_t[hl * HEAD_DIM:(hl + 1) * HEAD_DIM, :])
    o_ref[...] = jnp.concatenate(outs, axis=0).T.astype(o_ref.dtype)


def _rel_bucket_table(n_buckets):
    d = np.arange(REL_MAX_DIST, dtype=np.int32)
    max_exact = n_buckets // 2
    ratio = np.log(np.maximum(d, 1).astype(np.float32) / np.float32(max_exact)) / np.float32(
        math.log(REL_MAX_DIST / max_exact)) * np.float32(n_buckets - max_exact)
    large = np.minimum(max_exact + ratio.astype(np.int32), n_buckets - 1)
    return np.where(d < max_exact, d, large)


def _band_bias(rel_bias):
    n_buckets, n_heads = rel_bias.shape
    t = ATT_TILE
    assert REL_MAX_DIST <= t
    rb = rel_bias.astype(_F32)
    shifted = rb - rb[n_buckets - 1][None, :]
    by_dist = shifted[_rel_bucket_table(n_buckets)].T
    dv = jnp.concatenate([jnp.zeros((n_heads, t - 1), _F32), by_dist,
                          jnp.zeros((n_heads, 2 * t - REL_MAX_DIST + 1), _F32)], axis=1)
    slots = []
    for off in (t, 0):
        v = dv[:, off:off + 2 * t]
        flat = jnp.tile(v, (1, t))[:, t - 1:t - 1 + t * (2 * t - 1)]
        slots.append(flat.reshape(n_heads, t, 2 * t - 1)[:, :, :t])
    return jnp.stack(slots, axis=1)


def _dsa_attention(proj2, proj1, rel_bias, batch, seq, d_att, d_conv, n_idx_heads):
    t = ATT_TILE
    n_heads = d_att // HEAD_DIM
    n_pairs = n_heads // 2
    idx_w = n_idx_heads * IDX_DIM
    assert seq % t == 0 and n_heads % 2 == 0 and (3 * d_att) % idx_w == 0 and (2 * d_conv) % LANES == 0
    assert IDX_DIM + n_idx_heads <= LANES
    nq = seq // t
    n_sel = min(TOPK_MAX, seq // 4)
    assert n_sel <= t and seq & (seq - 1) == 0 and nq >= 2 and nq % ATT_CHUNK == 0
    cb = d_att // LANES
    kwb = 2 * d_conv // LANES
    tb = _band_bias(rel_bias)
    return pl.pallas_call(
        functools.partial(_dsa_kernel, n_idx_heads=n_idx_heads, n_sel=n_sel, n_pos=seq),
        grid=(batch, nq, n_pairs),
        in_specs=[pl.BlockSpec((t, LANES), lambda b, i, p: (b * nq + i, p)),
                  pl.BlockSpec((seq, LANES), lambda b, i, p: (b, cb + p)),
                  pl.BlockSpec((seq, LANES), lambda b, i, p: (b, 2 * cb + p)),
                  pl.BlockSpec((t, idx_w), lambda b, i, p: (b * nq + i, 3 * d_att // idx_w)),
                  pl.BlockSpec((seq, LANES), lambda b, i, p: (b, kwb)),
                  pl.BlockSpec((t, LANES), lambda b, i, p: (b * nq + i, kwb)),
                  pl.BlockSpec((2, 2, t, t), lambda b, i, p: (p, 0, 0, 0))],
        out_specs=pl.BlockSpec((t, LANES), lambda b, i, p: (b * nq + i, p)),
        out_shape=jax.ShapeDtypeStruct((batch * seq, d_att), _BF16),
        scratch_shapes=[pltpu.VMEM((seq, IDX_DIM), _BF16),
                        pltpu.VMEM((nq, t, t), jnp.int32),
                        pltpu.VMEM((nq, t, t), _F32),
                        pltpu.VMEM((2, t, t), _F32),
                        pltpu.VMEM((2, nq, t, t), _F32),
                        pltpu.VMEM((2, 2, t, t), _F32),
                        pltpu.VMEM((2, LANES, t), _F32),
                        pltpu.VMEM((1, t), jnp.int32)],
        compiler_params=_params("parallel", "arbitrary", "arbitrary"),
        name="dsa_attention",
    )(proj2, proj2, proj2, proj2, proj1, proj1, tb)


def _layer_norm(y, g, b):
    mu = jnp.mean(y, axis=-1, keepdims=True)
    d = y - mu
    var = jnp.mean(d * d, axis=-1, keepdims=True)
    return d * lax.rsqrt(var + LN_EPS) * g + b


def _outproj_kernel(c_ref, a_ref, w1_ref, w2_ref, b_ref, x_ref, g_ref, beta_ref, o_ref, ob_ref, ot_ref, *, alpha):
    mix = jnp.dot(c_ref[...], w1_ref[...], preferred_element_type=_F32)
    mix = mix + jnp.dot(a_ref[...], w2_ref[...], preferred_element_type=_F32) + b_ref[...]
    y = _layer_norm(alpha * x_ref[...] + mix, g_ref[...], beta_ref[...])
    o_ref[...] = y
    ob_ref[...] = y.astype(_BF16)
    ot_ref[...] = y.T.astype(_BF16)


def _out_projection(conv_out, att_out, w_out, b_out, x2, g, beta, alpha, tm):
    n, d = x2.shape
    dc = conv_out.shape[1]
    da = att_out.shape[1]
    vec = pl.BlockSpec((1, d), lambda i: (0, 0))
    return pl.pallas_call(
        functools.partial(_outproj_kernel, alpha=alpha),
        grid=(n // tm,),
        in_specs=[pl.BlockSpec((tm, dc), lambda i: (i, 0)),
                  pl.BlockSpec((tm, da), lambda i: (i, 0)),
                  pl.BlockSpec((dc, d), lambda i: (0, 0)),
                  pl.BlockSpec((da, d), lambda i: (0, 0)),
                  vec,
                  pl.BlockSpec((tm, d), lambda i: (i, 0)),
                  vec, vec],
        out_specs=[pl.BlockSpec((tm, d), lambda i: (i, 0)), pl.BlockSpec((tm, d), lambda i: (i, 0)),
                   pl.BlockSpec((d, tm), lambda i: (0, i))],
        out_shape=[jax.ShapeDtypeStruct((n, d), _F32), jax.ShapeDtypeStruct((n, d), _BF16),
                   jax.ShapeDtypeStruct((d, n), _BF16)],
        compiler_params=_params("parallel"),
        name="out_projection_ln1",
    )(conv_out, att_out, w_out[:dc], w_out[dc:], b_out.reshape(1, d), x2, g.reshape(1, d), beta.reshape(1, d))


def _peer_pairs(k):
    return [(i, j) for i in range(k) for j in range(k) if (i + 1) * (j + 1) <= k]


PEER_POPS = PEER_TOPK + 1


def _batcher_pairs(n):
    pairs = []
    p = 1
    while p < n:
        k = p
        while k >= 1:
            for j in range(k % p, n - k, 2 * k):
                for i in range(min(k, n - j - k)):
                    if (i + j) // (2 * p) == (i + j + k) // (2 * p):
                        pairs.append((i + j, i + j + k))
            k //= 2
        p *= 2
    return pairs


def _sort_desc(xs):
    xs = list(xs)
    for i, j in _batcher_pairs(len(xs)):
        xs[i], xs[j] = jnp.maximum(xs[i], xs[j]), jnp.minimum(xs[i], xs[j])
    return xs


def _bitonic_merge_desc(xs):
    xs = list(xs)
    n = len(xs)
    k = n // 2
    while k >= 1:
        for i in range(n):
            if i & k == 0:
                xs[i], xs[i + k] = jnp.maximum(xs[i], xs[i + k]), jnp.minimum(xs[i], xs[i + k])
        k //= 2
    return xs


def _partner(xs, shift):
    return [pltpu.roll(x, shift, 0) for x in xs]


def _merge_keep(a, b, dropped):
    n = len(a)
    hi = [jnp.maximum(a[i], b[n - 1 - i]) for i in range(n)]
    for i in range(n):
        dropped = jnp.maximum(dropped, jnp.minimum(a[i], b[n - 1 - i]))
    return _bitonic_merge_desc(hi), dropped


def _top_across_sublanes(xs, keep):
    dropped = jnp.full(xs[0].shape, -jnp.inf, _F32)
    for shift in (4, 2, 1):
        other = _partner(xs, shift)
        if 2 * len(xs) <= keep:
            xs = _bitonic_merge_desc(xs + other[::-1])
        else:
            assert len(xs) == keep
            xs, dropped = _merge_keep(xs, other, jnp.maximum(dropped, pltpu.roll(dropped, shift, 0)))
    return xs, dropped


def _pack_sublanes(slabs, sub):
    out = []
    for g in range(-(-len(slabs) // SUBLANES)):
        acc = jnp.full(slabs[0].shape, -jnp.inf, _F32)
        for s in range(SUBLANES):
            if g * SUBLANES + s < len(slabs):
                acc = jnp.where(sub == s, slabs[g * SUBLANES + s], acc)
        out.append(acc)
    return out


def _peer_select_kernel(x_ref, wq_ref, sk_ref, s2_ref, t_ref, e1_ref, e2_ref, *, n_heads, n_keys):
    tm = x_ref.shape[0]
    dk = sk_ref.shape[2]
    q = jnp.dot(x_ref[...], wq_ref[...], preferred_element_type=_F32).astype(_BF16)
    sub = lax.broadcasted_iota(jnp.int32, (SUBLANES, tm), 0)
    pairs = _peer_pairs(PEER_POPS)
    for h in range(n_heads):
        tops = []
        scores = []
        for c in range(2):
            g = 2 * h + c
            s_t = lax.dot_general(sk_ref[g], q[:, g * dk:(g + 1) * dk], _NT,
                                  preferred_element_type=_F32)
            scores.append(s_t)
            groups = _sort_desc([s_t[r * SUBLANES:(r + 1) * SUBLANES, :] for r in range(n_keys // SUBLANES)])
            top, nxt = _top_across_sublanes(groups, PEER_TOPK)
            tops.append(top + [nxt])
        a, b = tops
        cand = _pack_sublanes([a[i] + b[j] for (i, j) in pairs], sub)
        cand += [jnp.full((SUBLANES, tm), -jnp.inf, _F32)] * (SUBLANES - len(cand))
        top, nxt = _top_across_sublanes(_sort_desc(cand), PEER_TOPK)
        vals = [v[0:1, :] for v in top + [nxt]]
        a = [v[0:1, :] for v in a]
        b = [v[0:1, :] for v in b]
        z = jnp.zeros((1, tm), _F32)
        for v in vals[:PEER_TOPK]:
            z = z + jnp.exp(v - vals[0])
        theta = 0.5 * (vals[PEER_TOPK - 1] + vals[PEER_TOPK])
        s2_ref[h] = scores[1]
        t_ref[h] = theta - scores[0]
        e1_ref[h] = jnp.exp(scores[0] - a[0])
        e2_ref[h] = jnp.exp(scores[1] - b[0]) / z


def _peer_select(x1b, wq, sub_keys, tm):
    n, d = x1b.shape
    n_heads, _, n_keys, dk = sub_keys.shape
    assert n_keys == PEER_TOPK * SUBLANES
    sk = sub_keys.reshape(2 * n_heads, n_keys, dk).astype(_BF16)
    g2 = 2 * n_heads
    out = pl.BlockSpec((n_heads, n_keys, tm), lambda i: (0, 0, i))
    shape = jax.ShapeDtypeStruct((n_heads, n_keys, n), _F32)
    return pl.pallas_call(
        functools.partial(_peer_select_kernel, n_heads=n_heads, n_keys=n_keys),
        grid=(n // tm,),
        in_specs=[pl.BlockSpec((tm, d), lambda i: (i, 0)),
                  pl.BlockSpec((d, g2 * dk), lambda i: (0, 0)),
                  pl.BlockSpec((g2, n_keys, dk), lambda i: (0, 0, 0))],
        out_specs=[out, out, out, out],
        out_shape=[shape, shape, shape, shape],
        compiler_params=_params("parallel"),
        name="peer_select",
    )(x1b, wq, sk)


PEER_SUB = 256


def _peer_expert_kernel(xt_ref, u_ref, vt_ref, s2_ref, t_ref, e1_ref, e2_ref, x_ref, g_ref, beta_ref, o_ref,
                        y_ref, a0_ref, a1_ref, *, n_heads, n_keys, alpha):
    k = pl.program_id(1)
    nk = pl.num_programs(1) - 1
    te = u_ref.shape[0]
    tm = xt_ref.shape[1]
    d = vt_ref.shape[0]
    kt = jnp.minimum(k, nk - 1)
    units = [(r, c) for r in range(te // n_keys) for c in range(tm // PEER_SUB)]
    n_pieces = d // PEER_SUB
    pieces_of = [[m for m in range(n_pieces) if m * len(units) // n_pieces == ui] for ui in range(len(units))]

    @pl.when(k == 0)
    def _():
        y_ref[...] = jnp.zeros_like(y_ref)
        a1_ref[...] = jnp.zeros_like(a1_ref)

    def body(a_cur, a_prev):
        def second_matmul_piece(m):
            rows = slice(m * PEER_SUB, (m + 1) * PEER_SUB)
            y_ref[rows, :] += jnp.dot(vt_ref[rows, :], a_prev[...], preferred_element_type=_F32)

        for ui, (r, c2) in enumerate(units):
            rows = slice(r * n_keys, (r + 1) * n_keys)
            h_t = jnp.dot(u_ref[rows, :], xt_ref[:, c2 * PEER_SUB:(c2 + 1) * PEER_SUB],
                          preferred_element_type=_F32)
            for m in pieces_of[ui]:
                second_matmul_piece(m)
            act = 0.5 * h_t * (1.0 + lax.erf(h_t * (2.0 ** -0.5)))
            i1 = kt * (te // n_keys) + r
            for c in range(PEER_SUB // LANES):
                cols = slice(c2 * PEER_SUB + c * LANES, c2 * PEER_SUB + (c + 1) * LANES)
                gate = jnp.zeros((n_keys, LANES), _F32)
                for h in range(n_heads):
                    sel = s2_ref[h, :, cols] >= t_ref[h, pl.ds(i1, 1), :][:, cols]
                    gate = gate + jnp.where(sel, e2_ref[h, :, cols] * e1_ref[h, pl.ds(i1, 1), :][:, cols], 0.0)
                a_cur[rows, cols] = (gate * act[:, c * LANES:(c + 1) * LANES]).astype(a_cur.dtype)

    @pl.when(k % 2 == 0)
    def _():
        body(a0_ref, a1_ref)

    @pl.when(k % 2 == 1)
    def _():
        body(a1_ref, a0_ref)

    @pl.when(k == nk)
    def _():
        y = alpha * x_ref[...] + y_ref[...].T
        o_ref[...] = _layer_norm(y, g_ref[...], beta_ref[...])


def _peer_experts(x1, x1t, u_b, v_tab, s2, t, e1, e2, g, beta, alpha, tm, te):
    n, d = x1.shape
    n_exp = u_b.shape[0]
    n_heads, n_keys, _ = s2.shape
    assert n_exp == n_keys * n_keys and te % n_keys == 0 and d % PEER_SUB == 0
    assert n_exp % te == 0 and n % tm == 0 and tm % PEER_SUB == 0
    nk = n_exp // te
    vt_b = jnp.transpose(v_tab.reshape(nk, te, d), (0, 2, 1)).astype(_BF16)
    vec = pl.BlockSpec((1, d), lambda j, k: (0, 0))
    sel = pl.BlockSpec((n_heads, n_keys, tm), lambda j, k: (0, 0, j))
    once = dict(pipeline_mode=pl.Buffered(1))
    return pl.pallas_call(
        functools.partial(_peer_expert_kernel, n_heads=n_heads, n_keys=n_keys, alpha=alpha),
        grid=(n // tm, nk + 1),
        in_specs=[pl.BlockSpec((d, tm), lambda j, k: (0, j)),
                  pl.BlockSpec((te, d), lambda j, k: (jnp.minimum(k, nk - 1), 0)),
                  pl.BlockSpec((None, d, te), lambda j, k: (jnp.maximum(k - 1, 0), 0, 0)),
                  sel, sel, sel, sel,
                  pl.BlockSpec((tm, d), lambda j, k: (j, 0), **once),
                  vec, vec],
        out_specs=pl.BlockSpec((tm, d), lambda j, k: (j, 0), **once),
        out_shape=jax.ShapeDtypeStruct((n, d), _F32),
        scratch_shapes=[pltpu.VMEM((d, tm), _F32), pltpu.VMEM((te, tm), _BF16), pltpu.VMEM((te, tm), _BF16)],
        compiler_params=_params("parallel", "arbitrary"),
        name="peer_experts_ln2",
    )(x1t, u_b, vt_b, s2, t, e1, e2, x1, g.reshape(1, d), beta.reshape(1, d))


def _pick(n, pref):
    t = min(pref, n)
    while n % t:
        t //= 2
    return t


def kernel(x, w_in, w_out, b_out, dw_w, dw_b, conv_ln_g, conv_ln_b, rel_bias, ln1_g, ln1_b,
           peer_wq, peer_sub_keys, peer_u, peer_v, ln2_g, ln2_b):
    batch, seq, d = x.shape
    depth = w_in.shape[0]
    alpha = (2.0 * depth) ** 0.25
    d_conv = dw_w.shape[2]
    d_att = d - d_conv
    d_in = w_in.shape[2]
    n_idx_heads = (d_in - 2 * d_conv - 3 * d_att - IDX_DIM) // (IDX_DIM + 1)
    idx_w = n_idx_heads * IDX_DIM
    assert 2 * d_conv + 3 * d_att + idx_w + IDX_DIM + n_idx_heads == d_in
    n = batch * seq
    x2 = x.reshape(n, d)
    for l in range(depth):
        c0 = 2 * d_conv
        c1 = c0 + 3 * d_att + idx_w
        tail = jnp.pad(w_in[l][:, c1:], ((0, 0), (0, LANES - (d_in - c1))))
        w1 = jnp.concatenate([w_in[l][:, :c0], tail], axis=1).astype(_BF16)
        w2 = w_in[l][:, c0:c1].astype(_BF16)
        tm = _pick(n, 1024)
        proj1 = _matmul(x2, w1, _F32, tm, w1.shape[1])
        proj2 = _matmul(x2, w2, _BF16, tm, _pick(w2.shape[1], 1024))
        conv_out = _conv_group(proj1, dw_w[l], dw_b[l], conv_ln_g[l], conv_ln_b[l], batch, seq, d_conv,
                               _pick(seq, 256))
        att_out = _dsa_attention(proj2, proj1, rel_bias, batch, seq, d_att, d_conv, n_idx_heads)
        x1, x1b, x1t = _out_projection(conv_out, att_out, w_out[l].astype(_BF16), b_out[l], x2,
                                       ln1_g[l], ln1_b[l], alpha, _pick(n, 512))
        s2, t, e1, e2 = _peer_select(x1b, peer_wq[l].astype(_BF16), peer_sub_keys[l], _pick(n, 256))
        x2 = _peer_experts(x1, x1t, peer_u[l].astype(_BF16), peer_v[l], s2, t, e1, e2,
                           ln2_g[l], ln2_b[l], alpha, _pick(n, 512), 512)
    return x2.reshape(batch, seq, d)
```

```python
import functools
import math

import numpy as np
import jax
import jax.numpy as jnp
from jax import lax
from jax.experimental import pallas as pl
from jax.experimental.pallas import tpu as pltpu

HEAD_DIM = 64
IDX_DIM = 64
TOPK_MAX = 256
REL_MAX_DIST = 128
PEER_TOPK = 16
LN_EPS = 1e-5

LANES = 128
SUBLANES = 8
VMEM_LIMIT_BYTES = 56 * 1024 * 1024
NEG_BIG = -1e30
INT_MIN = -(2 ** 31)

_BF16 = jnp.bfloat16
_F32 = jnp.float32
_NT = (((1,), (1,)), ((), ()))
_TN = (((0,), (0,)), ((), ()))


def _params(*sem, flags=None):
    return pltpu.CompilerParams(dimension_semantics=sem, vmem_limit_bytes=VMEM_LIMIT_BYTES, flags=flags)


def _matmul_kernel(x_ref, w_ref, o_ref, xb_ref):
    @pl.when(pl.program_id(1) == 0)
    def _():
        xb_ref[...] = x_ref[...].astype(_BF16)

    o_ref[...] = jnp.dot(xb_ref[...], w_ref[...], preferred_element_type=_F32).astype(o_ref.dtype)


def _matmul(x, w, out_dtype, tm, tn):
    m, k = x.shape
    n = w.shape[1]
    assert m % tm == 0 and n % tn == 0
    return pl.pallas_call(
        _matmul_kernel,
        grid=(m // tm, n // tn),
        in_specs=[pl.BlockSpec((tm, k), lambda i, j: (i, 0)),
                  pl.BlockSpec((k, tn), lambda i, j: (0, j))],
        out_specs=pl.BlockSpec((tm, tn), lambda i, j: (i, j)),
        out_shape=jax.ShapeDtypeStruct((m, n), out_dtype),
        scratch_shapes=[pltpu.VMEM((tm, k), _BF16)],
        compiler_params=_params("parallel", "arbitrary"),
        name="proj_matmul",
    )(x, w)


CONV_HALO = 32
CONV_ROWS = 16
CONV_UNROLL = 4


def _conv_kernel(a1_ref, a2_ref, h1_ref, h2_ref, w_ref, b_ref, g_ref, beta_ref, o_ref, hs_ref, rot_ref,
                 *, t_rows, width):
    first = pl.program_id(1) == 0
    halo = h1_ref[...] * jax.nn.sigmoid(h2_ref[...])
    hs_ref[0:CONV_HALO, :] = jnp.where(first, 0.0, halo)
    hs_ref[CONV_HALO:CONV_HALO + t_rows, :] = a1_ref[...] * jax.nn.sigmoid(a2_ref[...])
    n_rot = t_rows + CONV_HALO - SUBLANES
    rot_ref[0, :, :] = hs_ref[...]
    for r in range(1, SUBLANES):
        rot_ref[r, 0:n_rot, :] = hs_ref[r:r + n_rot, :]
    base = CONV_HALO - (width - 1)
    inv_c = 1.0 / o_ref.shape[-1]

    def chunk(c, carry):
        row0 = pl.multiple_of(c * CONV_ROWS, CONV_ROWS)
        acc = jnp.broadcast_to(b_ref[...], (CONV_ROWS, o_ref.shape[-1]))
        for j in range(width):
            q, r = divmod(base + j, SUBLANES)
            tap = rot_ref[r, pl.ds(row0 + q * SUBLANES, CONV_ROWS), :]
            acc = acc + w_ref[j:j + 1, :] * tap
        mu = jnp.sum(acc, axis=-1, keepdims=True) * inv_c
        d = acc - mu
        var = jnp.sum(d * d, axis=-1, keepdims=True) * inv_c
        y = d * lax.rsqrt(var + LN_EPS) * g_ref[...] + beta_ref[...]
        o_ref[pl.ds(row0, CONV_ROWS), :] = (y * jax.nn.sigmoid(y)).astype(o_ref.dtype)
        return carry

    lax.fori_loop(0, t_rows // CONV_ROWS, chunk, 0, unroll=CONV_UNROLL)


def _conv_group(proj1, dw_w, dw_b, cn_g, cn_b, batch, seq, d_conv, t_rows):
    width = dw_w.shape[0]
    assert width - 1 <= CONV_HALO and seq % t_rows == 0 and t_rows % CONV_HALO == 0
    nt = seq // t_rows
    hb = t_rows // CONV_HALO

    def cur(col):
        return pl.BlockSpec((t_rows, d_conv), lambda b, s: (b * nt + s, col))

    def halo(col):
        return pl.BlockSpec((CONV_HALO, d_conv), lambda b, s: (jnp.maximum((b * nt + s) * hb - 1, 0), col))

    vec = pl.BlockSpec((1, d_conv), lambda b, s: (0, 0))
    return pl.pallas_call(
        functools.partial(_conv_kernel, t_rows=t_rows, width=width),
        grid=(batch, nt),
        in_specs=[cur(0), cur(1), halo(0), halo(1),
                  pl.BlockSpec((width, d_conv), lambda b, s: (0, 0)), vec, vec, vec],
        out_specs=pl.BlockSpec((t_rows, d_conv), lambda b, s: (b * nt + s, 0)),
        out_shape=jax.ShapeDtypeStruct((batch * seq, d_conv), _BF16),
        scratch_shapes=[pltpu.VMEM((t_rows + CONV_HALO, d_conv), _F32),
                        pltpu.VMEM((SUBLANES, t_rows + CONV_HALO, d_conv), _F32)],
        compiler_params=_params("parallel", "arbitrary"),
        name="conv_group",
    )(proj1, proj1, proj1, proj1, dw_w, dw_b.reshape(1, -1), cn_g.reshape(1, -1), cn_b.reshape(1, -1))


ATT_TILE = 256
ATT_CHUNK = 4


def _sortable_key(x):
    b = pltpu.bitcast(x, jnp.int32)
    return b ^ ((b >> 31) & jnp.int32(0x7FFFFFFF))


def _fold8(x, op):
    r, c = x.shape
    x = x.reshape(r // SUBLANES, SUBLANES, c)
    return jnp.sum(x, axis=0) if op == "sum" else jnp.max(x, axis=0)


def _dsa_kernel(q_ref, k_ref, v_ref, qi_ref, kw_ref, kwq_ref, tb_ref, o_ref,
                kib_ref, keys_ref, maskb_ref, maskn_ref, lg_ref, lgn_ref, acc_ref, cut_ref,
                *, n_idx_heads, n_sel, n_pos):
    t = ATT_TILE
    i = pl.program_id(1)
    p = pl.program_id(2)
    near_base = jnp.maximum(i - 1, 0)
    n_far = near_base
    n_far_chunks = lax.shift_right_logical(n_far + (ATT_CHUNK - 1), ATT_CHUNK.bit_length() - 1)
    row = lax.broadcasted_iota(jnp.int32, (t, t), 0)
    col = lax.broadcasted_iota(jnp.int32, (t, t), 1)

    def causal(kt):
        return (kt * t + row) <= (i * t + col)

    @pl.when(p == 0)
    def _select():
        @pl.when(i == 0)
        def _():
            kib_ref[...] = kw_ref[:, 0:IDX_DIM].astype(_BF16)

        w_t = kwq_ref[...].T

        def score_tile(kt, carry):
            ki = kib_ref[pl.ds(pl.multiple_of(kt * t, t), t), :]
            acc = jnp.zeros((t, t), _F32)
            for h in range(n_idx_heads):
                z = lax.dot_general(ki, qi_ref[:, h * IDX_DIM:(h + 1) * IDX_DIM], _NT,
                                    preferred_element_type=_F32)
                acc = acc + jnp.maximum(z, 0.0) * w_t[IDX_DIM + h:IDX_DIM + h + 1, :]
            acc = jnp.where(causal(kt), acc, -jnp.inf)
            keys_ref[kt] = _sortable_key(acc)
            return carry

        lax.fori_loop(0, i + 1, score_tile, 0)

        def count_ge(cand):
            def body(kt, pc):
                return pc + _fold8((keys_ref[kt] >= cand).astype(jnp.int32), "sum")
            pc = lax.fori_loop(0, i + 1, body, jnp.zeros((SUBLANES, t), jnp.int32))
            return jnp.sum(pc, axis=0, keepdims=True)

        zero = jnp.zeros((1, t), jnp.int32)
        tau0 = jnp.where(count_ge(zero) >= n_sel, zero, jnp.int32(INT_MIN))

        def bit_step(it, tau):
            cand = tau + (jnp.int32(1) << (30 - it))
            return jnp.where(count_ge(cand) >= n_sel, cand, tau)

        tau = lax.fori_loop(0, 31, bit_step, tau0)

        n_gt = count_ge(tau + 1)
        need = n_sel - n_gt
        excess = count_ge(tau) - n_gt - need
        cut_ref[...] = jnp.full((1, t), n_pos, jnp.int32)

        @pl.when(jnp.max(excess) > 0)
        def _():
            def count_eq_below(bound):
                def body(kt, pc):
                    hit = (keys_ref[kt] == tau) & ((kt * t + row) < bound)
                    return pc + _fold8(hit.astype(jnp.int32), "sum")
                pc = lax.fori_loop(0, i + 1, body, jnp.zeros((SUBLANES, t), jnp.int32))
                return jnp.sum(pc, axis=0, keepdims=True)

            def pos_step(it, pos):
                cand = pos + (jnp.int32(n_pos) >> (it + 1))
                return jnp.where(count_eq_below(cand) < need, cand, pos)

            cut_ref[...] = lax.fori_loop(0, n_pos.bit_length() - 1, pos_step, jnp.zeros((1, t), jnp.int32))

        cut = cut_ref[...]

        def mask_tile(kt, carry):
            key = keys_ref[kt]
            sel = ((key > tau) | ((key == tau) & ((kt * t + row) <= cut))) & causal(kt)
            maskb_ref[kt] = jnp.where(sel, 0.0, NEG_BIG)
            return carry

        lax.fori_loop(0, i + 1, mask_tile, 0)
        neg_tile = jnp.full((t, t), NEG_BIG, _F32)
        maskn_ref[0] = maskb_ref[near_base]
        maskn_ref[1] = jnp.where(i > 0, maskb_ref[i], neg_tile)

        def pad_tile(kt, carry):
            maskb_ref[kt] = neg_tile
            return carry

        lax.fori_loop(n_far, n_far_chunks * ATT_CHUNK, pad_tile, 0)

    ck = ATT_CHUNK * t
    lane = lax.broadcasted_iota(jnp.int32, (t, LANES), 1)
    q_all = q_ref[...] * jnp.asarray(HEAD_DIM ** -0.5, _BF16)
    qm = [jnp.where((lane >= hl * HEAD_DIM) & (lane < (hl + 1) * HEAD_DIM), q_all, jnp.zeros_like(q_all))
          for hl in range(2)]
    near_rows = pl.ds(pl.multiple_of(near_base * t, t), 2 * t)
    near_bias = (jnp.where(i > 0, 0, 1), 1)

    def far_logits(c, mps):
        kk = k_ref[pl.ds(pl.multiple_of(c * ck, ck), ck), :]
        new = []
        for hl in range(2):
            st = lax.dot_general(kk, qm[hl], _NT, preferred_element_type=_F32)
            mp = mps[hl]
            for u in range(ATT_CHUNK):
                tile = st[u * t:(u + 1) * t, :] + maskb_ref[c * ATT_CHUNK + u]
                lg_ref[hl, c * ATT_CHUNK + u] = tile
                mp = jnp.maximum(mp, _fold8(tile, "max"))
            new.append(mp)
        return tuple(new)

    mp0 = jnp.full((SUBLANES, t), NEG_BIG, _F32)
    mps = lax.fori_loop(0, n_far_chunks, far_logits, (mp0, mp0))
    kk = k_ref[near_rows, :]
    m = []
    for hl in range(2):
        st = lax.dot_general(kk, qm[hl], _NT, preferred_element_type=_F32)
        mp = mps[hl]
        for s in range(2):
            tile = st[s * t:(s + 1) * t, :] + maskn_ref[s] + tb_ref[hl, near_bias[s]]
            lgn_ref[hl, s] = tile
            mp = jnp.maximum(mp, _fold8(tile, "max"))
        m.append(jnp.max(mp, axis=0, keepdims=True))

    acc_ref[...] = jnp.zeros_like(acc_ref)

    def far_pv(c, lps):
        vv = v_ref[pl.ds(pl.multiple_of(c * ck, ck), ck), :]
        new = []
        for hl in range(2):
            pr = jnp.exp(lg_ref[hl, pl.ds(c * ATT_CHUNK, ATT_CHUNK)].reshape(ck, t) - m[hl])
            new.append(lps[hl] + _fold8(pr, "sum"))
            acc_ref[hl] += lax.dot_general(vv, pr.astype(_BF16), _TN, preferred_element_type=_F32)
        return tuple(new)

    lp0 = jnp.zeros((SUBLANES, t), _F32)
    lps = lax.fori_loop(0, n_far_chunks, far_pv, (lp0, lp0))
    vv = v_ref[near_rows, :]
    outs = []
    for hl in range(2):
        pr = jnp.exp(lgn_ref[hl].reshape(2 * t, t) - m[hl])
        lp = lps[hl] + _fold8(pr, "sum")
        o_t = acc_ref[hl] + lax.dot_general(vv, pr.astype(_BF16), _TN, preferred_element_type=_F32)
        o_t = o_t / jnp.sum(lp, axis=0, keepdims=True)
        outs.append(o_t[hl * HEAD_DIM:(hl + 1) * HEAD_DIM, :])
    o_ref[...] = jnp.concatenate(outs, axis=0).T.astype(o_ref.dtype)


def _rel_bucket_table(n_buckets):
    d = np.arange(REL_MAX_DIST, dtype=np.int32)
    max_exact = n_buckets // 2
    ratio = np.log(np.maximum(d, 1).astype(np.float32) / np.float32(max_exact)) / np.float32(
        math.log(REL_MAX_DIST / max_exact)) * np.float32(n_buckets - max_exact)
    large = np.minimum(max_exact + ratio.astype(np.int32), n_buckets - 1)
    return np.where(d < max_exact, d, large)


def _band_bias(rel_bias):
    n_buckets, n_heads = rel_bias.shape
    t = ATT_TILE
    assert REL_MAX_DIST <= t
    rb = rel_bias.astype(_F32)
    shifted = rb - rb[n_buckets - 1][None, :]
    by_dist = shifted[_rel_bucket_table(n_buckets)].T
    dv = jnp.concatenate([jnp.zeros((n_heads, t - 1), _F32), by_dist,
                          jnp.zeros((n_heads, 2 * t - REL_MAX_DIST + 1), _F32)], axis=1)
    slots = []
    for off in (t, 0):
        v = dv[:, off:off + 2 * t]
        flat = jnp.tile(v, (1, t))[:, t - 1:t - 1 + t * (2 * t - 1)]
        slots.append(flat.reshape(n_heads, t, 2 * t - 1)[:, :, :t])
    return jnp.stack(slots, axis=1)


def _dsa_attention(proj2, proj1, rel_bias, batch, seq, d_att, d_conv, n_idx_heads):
    t = ATT_TILE
    n_heads = d_att // HEAD_DIM
    n_pairs = n_heads // 2
    idx_w = n_idx_heads * IDX_DIM
    assert seq % t == 0 and n_heads % 2 == 0 and (3 * d_att) % idx_w == 0 and (2 * d_conv) % LANES == 0
    assert IDX_DIM + n_idx_heads <= LANES
    nq = seq // t
    n_sel = min(TOPK_MAX, seq // 4)
    assert n_sel <= t and seq & (seq - 1) == 0 and nq >= 2 and nq % ATT_CHUNK == 0
    cb = d_att // LANES
    kwb = 2 * d_conv // LANES
    tb = _band_bias(rel_bias)
    return pl.pallas_call(
        functools.partial(_dsa_kernel, n_idx_heads=n_idx_heads, n_sel=n_sel, n_pos=seq),
        grid=(batch, nq, n_pairs),
        in_specs=[pl.BlockSpec((t, LANES), lambda b, i, p: (b * nq + i, p)),
                  pl.BlockSpec((seq, LANES), lambda b, i, p: (b, cb + p)),
                  pl.BlockSpec((seq, LANES), lambda b, i, p: (b, 2 * cb + p)),
                  pl.BlockSpec((t, idx_w), lambda b, i, p: (b * nq + i, 3 * d_att // idx_w)),
                  pl.BlockSpec((seq, LANES), lambda b, i, p: (b, kwb)),
                  pl.BlockSpec((t, LANES), lambda b, i, p: (b * nq + i, kwb)),
                  pl.BlockSpec((2, 2, t, t), lambda b, i, p: (p, 0, 0, 0))],
        out_specs=pl.BlockSpec((t, LANES), lambda b, i, p: (b * nq + i, p)),
        out_shape=jax.ShapeDtypeStruct((batch * seq, d_att), _BF16),
        scratch_shapes=[pltpu.VMEM((seq, IDX_DIM), _BF16),
                        pltpu.VMEM((nq, t, t), jnp.int32),
                        pltpu.VMEM((nq, t, t), _F32),
                        pltpu.VMEM((2, t, t), _F32),
                        pltpu.VMEM((2, nq, t, t), _F32),
                        pltpu.VMEM((2, 2, t, t), _F32),
                        pltpu.VMEM((2, LANES, t), _F32),
                        pltpu.VMEM((1, t), jnp.int32)],
        compiler_params=_params("parallel", "arbitrary", "arbitrary"),
        name="dsa_attention",
    )(proj2, proj2, proj2, proj2, proj1, proj1, tb)


def _layer_norm(y, g, b):
    mu = jnp.mean(y, axis=-1, keepdims=True)
    d = y - mu
    var = jnp.mean(d * d, axis=-1, keepdims=True)
    return d * lax.rsqrt(var + LN_EPS) * g + b


def _outproj_kernel(c_ref, a_ref, w1_ref, w2_ref, b_ref, x_ref, g_ref, beta_ref, o_ref, ob_ref, ot_ref, *, alpha):
    mix = jnp.dot(c_ref[...], w1_ref[...], preferred_element_type=_F32)
    mix = mix + jnp.dot(a_ref[...], w2_ref[...], preferred_element_type=_F32) + b_ref[...]
    y = _layer_norm(alpha * x_ref[...] + mix, g_ref[...], beta_ref[...])
    o_ref[...] = y
    ob_ref[...] = y.astype(_BF16)
    ot_ref[...] = y.T.astype(_BF16)


def _out_projection(conv_out, att_out, w_out, b_out, x2, g, beta, alpha, tm):
    n, d = x2.shape
    dc = conv_out.shape[1]
    da = att_out.shape[1]
    vec = pl.BlockSpec((1, d), lambda i: (0, 0))
    return pl.pallas_call(
        functools.partial(_outproj_kernel, alpha=alpha),
        grid=(n // tm,),
        in_specs=[pl.BlockSpec((tm, dc), lambda i: (i, 0)),
                  pl.BlockSpec((tm, da), lambda i: (i, 0)),
                  pl.BlockSpec((dc, d), lambda i: (0, 0)),
                  pl.BlockSpec((da, d), lambda i: (0, 0)),
                  vec,
                  pl.BlockSpec((tm, d), lambda i: (i, 0)),
                  vec, vec],
        out_specs=[pl.BlockSpec((tm, d), lambda i: (i, 0)), pl.BlockSpec((tm, d), lambda i: (i, 0)),
                   pl.BlockSpec((d, tm), lambda i: (0, i))],
        out_shape=[jax.ShapeDtypeStruct((n, d), _F32), jax.ShapeDtypeStruct((n, d), _BF16),
                   jax.ShapeDtypeStruct((d, n), _BF16)],
        compiler_params=_params("parallel"),
        name="out_projection_ln1",
    )(conv_out, att_out, w_out[:dc], w_out[dc:], b_out.reshape(1, d), x2, g.reshape(1, d), beta.reshape(1, d))


def _peer_pairs(k):
    return [(i, j) for i in range(k) for j in range(k) if (i + 1) * (j + 1) <= k]


PEER_POPS = PEER_TOPK + 1


def _batcher_pairs(n):
    pairs = []
    p = 1
    while p < n:
        k = p
        while k >= 1:
            for j in range(k % p, n - k, 2 * k):
                for i in range(min(k, n - j - k)):
                    if (i + j) // (2 * p) == (i + j + k) // (2 * p):
                        pairs.append((i + j, i + j + k))
            k //= 2
        p *= 2
    return pairs


def _sort_desc(xs):
    xs = list(xs)
    for i, j in _batcher_pairs(len(xs)):
        xs[i], xs[j] = jnp.maximum(xs[i], xs[j]), jnp.minimum(xs[i], xs[j])
    return xs


def _bitonic_merge_desc(xs):
    xs = list(xs)
    n = len(xs)
    k = n // 2
    while k >= 1:
        for i in range(n):
            if i & k == 0:
                xs[i], xs[i + k] = jnp.maximum(xs[i], xs[i + k]), jnp.minimum(xs[i], xs[i + k])
        k //= 2
    return xs


def _partner(xs, shift):
    return [pltpu.roll(x, shift, 0) for x in xs]


def _merge_keep(a, b, dropped):
    n = len(a)
    hi = [jnp.maximum(a[i], b[n - 1 - i]) for i in range(n)]
    for i in range(n):
        dropped = jnp.maximum(dropped, jnp.minimum(a[i], b[n - 1 - i]))
    return _bitonic_merge_desc(hi), dropped


def _top_across_sublanes(xs, keep):
    dropped = jnp.full(xs[0].shape, -jnp.inf, _F32)
    for shift in (4, 2, 1):
        other = _partner(xs, shift)
        if 2 * len(xs) <= keep:
            xs = _bitonic_merge_desc(xs + other[::-1])
        else:
            assert len(xs) == keep
            xs, dropped = _merge_keep(xs, other, jnp.maximum(dropped, pltpu.roll(dropped, shift, 0)))
    return xs, dropped


def _pack_sublanes(slabs, sub):
    out = []
    for g in range(-(-len(slabs) // SUBLANES)):
        acc = jnp.full(slabs[0].shape, -jnp.inf, _F32)
        for s in range(SUBLANES):
            if g * SUBLANES + s < len(slabs):
                acc = jnp.where(sub == s, slabs[g * SUBLANES + s], acc)
        out.append(acc)
    return out


def _peer_select_kernel(x_ref, wq_ref, sk_ref, s2_ref, t_ref, e1_ref, e2_ref, *, n_heads, n_keys):
    tm = x_ref.shape[0]
    dk = sk_ref.shape[2]
    q = jnp.dot(x_ref[...], wq_ref[...], preferred_element_type=_F32).astype(_BF16)
    sub = lax.broadcasted_iota(jnp.int32, (SUBLANES, tm), 0)
    pairs = _peer_pairs(PEER_POPS)
    for h in range(n_heads):
        tops = []
        scores = []
        for c in range(2):
            g = 2 * h + c
            s_t = lax.dot_general(sk_ref[g], q[:, g * dk:(g + 1) * dk], _NT,
                                  preferred_element_type=_F32)
            scores.append(s_t)
            groups = _sort_desc([s_t[r * SUBLANES:(r + 1) * SUBLANES, :] for r in range(n_keys // SUBLANES)])
            top, nxt = _top_across_sublanes(groups, PEER_TOPK)
            tops.append(top + [nxt])
        a, b = tops
        cand = _pack_sublanes([a[i] + b[j] for (i, j) in pairs], sub)
        cand += [jnp.full((SUBLANES, tm), -jnp.inf, _F32)] * (SUBLANES - len(cand))
        top, nxt = _top_across_sublanes(_sort_desc(cand), PEER_TOPK)
        vals = [v[0:1, :] for v in top + [nxt]]
        a = [v[0:1, :] for v in a]
        b = [v[0:1, :] for v in b]
        z = jnp.zeros((1, tm), _F32)
        for v in vals[:PEER_TOPK]:
            z = z + jnp.exp(v - vals[0])
        theta = 0.5 * (vals[PEER_TOPK - 1] + vals[PEER_TOPK])
        s2_ref[h] = scores[1]
        t_ref[h] = theta - scores[0]
        e1_ref[h] = jnp.exp(scores[0] - a[0])
        e2_ref[h] = jnp.exp(scores[1] - b[0]) / z


def _peer_select(x1b, wq, sub_keys, tm):
    n, d = x1b.shape
    n_heads, _, n_keys, dk = sub_keys.shape
    assert n_keys == PEER_TOPK * SUBLANES
    sk = sub_keys.reshape(2 * n_heads, n_keys, dk).astype(_BF16)
    g2 = 2 * n_heads
    out = pl.BlockSpec((n_heads, n_keys, tm), lambda i: (0, 0, i))
    shape = jax.ShapeDtypeStruct((n_heads, n_keys, n), _F32)
    return pl.pallas_call(
        functools.partial(_peer_select_kernel, n_heads=n_heads, n_keys=n_keys),
        grid=(n // tm,),
        in_specs=[pl.BlockSpec((tm, d), lambda i: (i, 0)),
                  pl.BlockSpec((d, g2 * dk), lambda i: (0, 0)),
                  pl.BlockSpec((g2, n_keys, dk), lambda i: (0, 0, 0))],
        out_specs=[out, out, out, out],
        out_shape=[shape, shape, shape, shape],
        compiler_params=_params("parallel"),
        name="peer_select",
    )(x1b, wq, sk)


PEER_SUB = 256


def _peer_expert_kernel(xt_ref, u_ref, vt_ref, s2_ref, t_ref, e1_ref, e2_ref, x_ref, g_ref, beta_ref, o_ref,
                        y_ref, a0_ref, a1_ref, *, n_heads, n_keys, alpha, nk):
    k = pl.program_id(1)
    te = u_ref.shape[0]
    tm = xt_ref.shape[1]
    d = vt_ref.shape[0]
    kt = k
    units = [(r, c) for r in range(te // n_keys) for c in range(tm // PEER_SUB)]
    n_pieces = d // PEER_SUB
    pieces_of = [[m for m in range(n_pieces) if m * len(units) // n_pieces == ui] for ui in range(len(units))]

    def body(a_cur, a_prev, first=True, second=True):
        def second_matmul_piece(m):
            rows = slice(m * PEER_SUB, (m + 1) * PEER_SUB)
            y_ref[rows, :] += jnp.dot(vt_ref[rows, :], a_prev[...], preferred_element_type=_F32)

        for ui, (r, c2) in enumerate(units):
            rows = slice(r * n_keys, (r + 1) * n_keys)
            if first:
                h_t = jnp.dot(u_ref[rows, :], xt_ref[:, c2 * PEER_SUB:(c2 + 1) * PEER_SUB],
                              preferred_element_type=_F32)
            if second:
                for m in pieces_of[ui]:
                    second_matmul_piece(m)
            if not first:
                continue
            act = 0.5 * h_t * (1.0 + lax.erf(h_t * (2.0 ** -0.5)))
            i1 = kt * (te // n_keys) + r
            for c in range(PEER_SUB // LANES):
                cols = slice(c2 * PEER_SUB + c * LANES, c2 * PEER_SUB + (c + 1) * LANES)
                gate = jnp.zeros((n_keys, LANES), _F32)
                for h in range(n_heads):
                    sel = s2_ref[h, :, cols] >= t_ref[h, pl.ds(i1, 1), :][:, cols]
                    gate = gate + jnp.where(sel, e2_ref[h, :, cols] * e1_ref[h, pl.ds(i1, 1), :][:, cols], 0.0)
                a_cur[rows, cols] = (gate * act[:, c * LANES:(c + 1) * LANES]).astype(a_cur.dtype)

    bufs = (a0_ref, a1_ref)

    @pl.when(k == 0)
    def _():
        y_ref[...] = jnp.zeros_like(y_ref)
        body(bufs[0], bufs[1], second=False)

    for parity in range(2):
        @pl.when((k > 0) & (k < nk) & (k % 2 == parity))
        def _():
            body(bufs[parity], bufs[1 - parity])

    @pl.when(k == nk)
    def _():
        body(bufs[nk % 2], bufs[(nk - 1) % 2], first=False)
        y = alpha * x_ref[...] + y_ref[...].T
        o_ref[...] = _layer_norm(y, g_ref[...], beta_ref[...])


def _peer_experts(x1, x1t, u_b, v_tab, s2, t, e1, e2, g, beta, alpha, tm, te):
    n, d = x1.shape
    n_exp = u_b.shape[0]
    n_heads, n_keys, _ = s2.shape
    assert n_exp == n_keys * n_keys and te % n_keys == 0 and d % PEER_SUB == 0
    assert n_exp % te == 0 and n % tm == 0 and tm % PEER_SUB == 0
    nk = n_exp // te
    vt_b = jnp.transpose(v_tab.reshape(nk, te, d), (0, 2, 1)).astype(_BF16)
    vec = pl.BlockSpec((1, d), lambda j, k: (0, 0))
    sel = pl.BlockSpec((n_heads, n_keys, tm), lambda j, k: (0, 0, j))
    once = dict(pipeline_mode=pl.Buffered(1))
    return pl.pallas_call(
        functools.partial(_peer_expert_kernel, n_heads=n_heads, n_keys=n_keys, alpha=alpha, nk=nk),
        grid=(n // tm, nk + 1),
        in_specs=[pl.BlockSpec((d, tm), lambda j, k: (0, j)),
                  pl.BlockSpec((te, d), lambda j, k: (jnp.minimum(k, nk - 1), 0)),
                  pl.BlockSpec((None, d, te), lambda j, k: (jnp.maximum(k - 1, 0), 0, 0)),
                  sel, sel, sel, sel,
                  pl.BlockSpec((tm, d), lambda j, k: (j, 0), **once),
                  vec, vec],
        out_specs=pl.BlockSpec((tm, d), lambda j, k: (j, 0), **once),
        out_shape=jax.ShapeDtypeStruct((n, d), _F32),
        scratch_shapes=[pltpu.VMEM((d, tm), _F32), pltpu.VMEM((te, tm), _BF16), pltpu.VMEM((te, tm), _BF16)],
        compiler_params=_params("parallel", "arbitrary"),
        name="peer_experts_ln2",
    )(x1t, u_b, vt_b, s2, t, e1, e2, x1, g.reshape(1, d), beta.reshape(1, d))


def _pick(n, pref):
    t = min(pref, n)
    while n % t:
        t //= 2
    return t


def kernel(x, w_in, w_out, b_out, dw_w, dw_b, conv_ln_g, conv_ln_b, rel_bias, ln1_g, ln1_b,
           peer_wq, peer_sub_keys, peer_u, peer_v, ln2_g, ln2_b):
    batch, seq, d = x.shape
    depth = w_in.shape[0]
    alpha = (2.0 * depth) ** 0.25
    d_conv = dw_w.shape[2]
    d_att = d - d_conv
    d_in = w_in.shape[2]
    n_idx_heads = (d_in - 2 * d_conv - 3 * d_att - IDX_DIM) // (IDX_DIM + 1)
    idx_w = n_idx_heads * IDX_DIM
    assert 2 * d_conv + 3 * d_att + idx_w + IDX_DIM + n_idx_heads == d_in
    n = batch * seq
    x2 = x.reshape(n, d)
    for l in range(depth):
        c0 = 2 * d_conv
        c1 = c0 + 3 * d_att + idx_w
        tail = jnp.pad(w_in[l][:, c1:], ((0, 0), (0, LANES - (d_in - c1))))
        w1 = jnp.concatenate([w_in[l][:, :c0], tail], axis=1).astype(_BF16)
        w2 = w_in[l][:, c0:c1].astype(_BF16)
        tm = _pick(n, 1024)
        proj1 = _matmul(x2, w1, _F32, tm, w1.shape[1])
        proj2 = _matmul(x2, w2, _BF16, tm, _pick(w2.shape[1], 1024))
        conv_out = _conv_group(proj1, dw_w[l], dw_b[l], conv_ln_g[l], conv_ln_b[l], batch, seq, d_conv,
                               _pick(seq, 256))
        att_out = _dsa_attention(proj2, proj1, rel_bias, batch, seq, d_att, d_conv, n_idx_heads)
        x1, x1b, x1t = _out_projection(conv_out, att_out, w_out[l].astype(_BF16), b_out[l], x2,
                                       ln1_g[l], ln1_b[l], alpha, _pick(n, 512))
        s2, t, e1, e2 = _peer_select(x1b, peer_wq[l].astype(_BF16), peer_sub_keys[l], _pick(n, 256))
        x2 = _peer_experts(x1, x1t, peer_u[l].astype(_BF16), peer_v[l], s2, t, e1, e2,
                           ln2_g[l], ln2_b[l], alpha, _pick(n, 512), 512)
    return x2.reshape(batch, seq, d)
```

```python
import functools
import math

import numpy as np
import jax
import jax.numpy as jnp
from jax import lax
from jax.experimental import pallas as pl
from jax.experimental.pallas import tpu as pltpu

HEAD_DIM = 64
IDX_DIM = 64
TOPK_MAX = 256
REL_MAX_DIST = 128
PEER_TOPK = 16
LN_EPS = 1e-5

LANES = 128
SUBLANES = 8
VMEM_LIMIT_BYTES = 56 * 1024 * 1024
NEG_BIG = -1e30
INT_MIN = -(2 ** 31)

_BF16 = jnp.bfloat16
_F32 = jnp.float32
_NT = (((1,), (1,)), ((), ()))
_TN = (((0,), (0,)), ((), ()))


def _params(*sem, flags=None):
    return pltpu.CompilerParams(dimension_semantics=sem, vmem_limit_bytes=VMEM_LIMIT_BYTES, flags=flags)


def _matmul_kernel(x_ref, w_ref, o_ref, xb_ref):
    @pl.when(pl.program_id(1) == 0)
    def _():
        xb_ref[...] = x_ref[...].astype(_BF16)

    o_ref[...] = jnp.dot(xb_ref[...], w_ref[...], preferred_element_type=_F32).astype(o_ref.dtype)


def _matmul(x, w, out_dtype, tm, tn):
    m, k = x.shape
    n = w.shape[1]
    assert m % tm == 0 and n % tn == 0
    return pl.pallas_call(
        _matmul_kernel,
        grid=(m // tm, n // tn),
        in_specs=[pl.BlockSpec((tm, k), lambda i, j: (i, 0)),
                  pl.BlockSpec((k, tn), lambda i, j: (0, j))],
        out_specs=pl.BlockSpec((tm, tn), lambda i, j: (i, j)),
        out_shape=jax.ShapeDtypeStruct((m, n), out_dtype),
        scratch_shapes=[pltpu.VMEM((tm, k), _BF16)],
        compiler_params=_params("parallel", "arbitrary"),
        name="proj_matmul",
    )(x, w)


CONV_HALO = 32
CONV_ROWS = 16
CONV_UNROLL = 4


def _conv_kernel(a1_ref, a2_ref, h1_ref, h2_ref, w_ref, b_ref, g_ref, beta_ref, o_ref, hs_ref, rot_ref,
                 *, t_rows, width):
    first = pl.program_id(1) == 0
    halo = h1_ref[...] * jax.nn.sigmoid(h2_ref[...])
    hs_ref[0:CONV_HALO, :] = jnp.where(first, 0.0, halo)
    hs_ref[CONV_HALO:CONV_HALO + t_rows, :] = a1_ref[...] * jax.nn.sigmoid(a2_ref[...])
    n_rot = t_rows + CONV_HALO - SUBLANES
    rot_ref[0, :, :] = hs_ref[...]
    for r in range(1, SUBLANES):
        rot_ref[r, 0:n_rot, :] = hs_ref[r:r + n_rot, :]
    base = CONV_HALO - (width - 1)
    inv_c = 1.0 / o_ref.shape[-1]

    def chunk(c, carry):
        row0 = pl.multiple_of(c * CONV_ROWS, CONV_ROWS)
        acc = jnp.broadcast_to(b_ref[...], (CONV_ROWS, o_ref.shape[-1]))
        for j in range(width):
            q, r = divmod(base + j, SUBLANES)
            tap = rot_ref[r, pl.ds(row0 + q * SUBLANES, CONV_ROWS), :]
            acc = acc + w_ref[j:j + 1, :] * tap
        mu = jnp.sum(acc, axis=-1, keepdims=True) * inv_c
        d = acc - mu
        var = jnp.sum(d * d, axis=-1, keepdims=True) * inv_c
        y = d * lax.rsqrt(var + LN_EPS) * g_ref[...] + beta_ref[...]
        o_ref[pl.ds(row0, CONV_ROWS), :] = (y * jax.nn.sigmoid(y)).astype(o_ref.dtype)
        return carry

    lax.fori_loop(0, t_rows // CONV_ROWS, chunk, 0, unroll=CONV_UNROLL)


def _conv_group(proj1, dw_w, dw_b, cn_g, cn_b, batch, seq, d_conv, t_rows):
    width = dw_w.shape[0]
    assert width - 1 <= CONV_HALO and seq % t_rows == 0 and t_rows % CONV_HALO == 0
    nt = seq // t_rows
    hb = t_rows // CONV_HALO

    def cur(col):
        return pl.BlockSpec((t_rows, d_conv), lambda b, s: (b * nt + s, col))

    def halo(col):
        return pl.BlockSpec((CONV_HALO, d_conv), lambda b, s: (jnp.maximum((b * nt + s) * hb - 1, 0), col))

    vec = pl.BlockSpec((1, d_conv), lambda b, s: (0, 0))
    return pl.pallas_call(
        functools.partial(_conv_kernel, t_rows=t_rows, width=width),
        grid=(batch, nt),
        in_specs=[cur(0), cur(1), halo(0), halo(1),
                  pl.BlockSpec((width, d_conv), lambda b, s: (0, 0)), vec, vec, vec],
        out_specs=pl.BlockSpec((t_rows, d_conv), lambda b, s: (b * nt + s, 0)),
        out_shape=jax.ShapeDtypeStruct((batch * seq, d_conv), _BF16),
        scratch_shapes=[pltpu.VMEM((t_rows + CONV_HALO, d_conv), _F32),
                        pltpu.VMEM((SUBLANES, t_rows + CONV_HALO, d_conv), _F32)],
        compiler_params=_params("parallel", "arbitrary"),
        name="conv_group",
    )(proj1, proj1, proj1, proj1, dw_w, dw_b.reshape(1, -1), cn_g.reshape(1, -1), cn_b.reshape(1, -1))


ATT_TILE = 256
ATT_CHUNK = 4


def _sortable_key(x):
    b = pltpu.bitcast(x, jnp.int32)
    return b ^ ((b >> 31) & jnp.int32(0x7FFFFFFF))


def _fold8(x, op):
    r, c = x.shape
    x = x.reshape(r // SUBLANES, SUBLANES, c)
    return jnp.sum(x, axis=0) if op == "sum" else jnp.max(x, axis=0)


def _dsa_kernel(q_ref, k_ref, v_ref, qi_ref, kw_ref, kwq_ref, tb_ref, o_ref,
                kib_ref, keys_ref, maskb_ref, maskn_ref, lg_ref, lgn_ref, acc_ref, cut_ref,
                *, n_idx_heads, n_sel, n_pos):
    t = ATT_TILE
    i = pl.program_id(1)
    p = pl.program_id(2)
    near_base = jnp.maximum(i - 1, 0)
    n_far = near_base
    n_far_chunks = lax.shift_right_logical(n_far + (ATT_CHUNK - 1), ATT_CHUNK.bit_length() - 1)
    row = lax.broadcasted_iota(jnp.int32, (t, t), 0)
    col = lax.broadcasted_iota(jnp.int32, (t, t), 1)

    def causal(kt):
        return (kt * t + row) <= (i * t + col)

    @pl.when(p == 0)
    def _select():
        @pl.when(i == 0)
        def _():
            kib_ref[...] = kw_ref[:, 0:IDX_DIM].astype(_BF16)

        w_t = kwq_ref[...].T

        def score_tile(kt, carry):
            ki = kib_ref[pl.ds(pl.multiple_of(kt * t, t), t), :]
            acc = jnp.zeros((t, t), _F32)
            for h in range(n_idx_heads):
                z = lax.dot_general(ki, qi_ref[:, h * IDX_DIM:(h + 1) * IDX_DIM], _NT,
                                    preferred_element_type=_F32)
                acc = acc + jnp.maximum(z, 0.0) * w_t[IDX_DIM + h:IDX_DIM + h + 1, :]
            acc = jnp.where(causal(kt), acc, -jnp.inf)
            keys_ref[kt] = _sortable_key(acc)
            return carry

        lax.fori_loop(0, i + 1, score_tile, 0)

        def count_ge(cand):
            def body(kt, pc):
                return pc + _fold8((keys_ref[kt] >= cand).astype(jnp.int32), "sum")
            pc = lax.fori_loop(0, i + 1, body, jnp.zeros((SUBLANES, t), jnp.int32))
            return jnp.sum(pc, axis=0, keepdims=True)

        zero = jnp.zeros((1, t), jnp.int32)
        tau0 = jnp.where(count_ge(zero) >= n_sel, zero, jnp.int32(INT_MIN))

        def bit_step(it, tau):
            cand = tau + (jnp.int32(1) << (30 - it))
            return jnp.where(count_ge(cand) >= n_sel, cand, tau)

        tau = lax.fori_loop(0, 31, bit_step, tau0)

        n_gt = count_ge(tau + 1)
        need = n_sel - n_gt
        excess = count_ge(tau) - n_gt - need
        cut_ref[...] = jnp.full((1, t), n_pos, jnp.int32)

        @pl.when(jnp.max(excess) > 0)
        def _():
            def count_eq_below(bound):
                def body(kt, pc):
                    hit = (keys_ref[kt] == tau) & ((kt * t + row) < bound)
                    return pc + _fold8(hit.astype(jnp.int32), "sum")
                pc = lax.fori_loop(0, i + 1, body, jnp.zeros((SUBLANES, t), jnp.int32))
                return jnp.sum(pc, axis=0, keepdims=True)

            def pos_step(it, pos):
                cand = pos + (jnp.int32(n_pos) >> (it + 1))
                return jnp.where(count_eq_below(cand) < need, cand, pos)

            cut_ref[...] = lax.fori_loop(0, n_pos.bit_length() - 1, pos_step, jnp.zeros((1, t), jnp.int32))

        cut = cut_ref[...]

        def mask_tile(kt, carry):
            key = keys_ref[kt]
            sel = ((key > tau) | ((key == tau) & ((kt * t + row) <= cut))) & causal(kt)
            maskb_ref[kt] = jnp.where(sel, 0.0, NEG_BIG)
            return carry

        lax.fori_loop(0, i + 1, mask_tile, 0)
        neg_tile = jnp.full((t, t), NEG_BIG, _F32)
        maskn_ref[0] = maskb_ref[near_base]
        maskn_ref[1] = jnp.where(i > 0, maskb_ref[i], neg_tile)

        def pad_tile(kt, carry):
            maskb_ref[kt] = neg_tile
            return carry

        lax.fori_loop(n_far, n_far_chunks * ATT_CHUNK, pad_tile, 0)

    ck = ATT_CHUNK * t
    lane = lax.broadcasted_iota(jnp.int32, (t, LANES), 1)
    q_all = q_ref[...] * jnp.asarray(HEAD_DIM ** -0.5, _BF16)
    qm = [jnp.where((lane >= hl * HEAD_DIM) & (lane < (hl + 1) * HEAD_DIM), q_all, jnp.zeros_like(q_all))
          for hl in range(2)]
    near_rows = pl.ds(pl.multiple_of(near_base * t, t), 2 * t)
    near_bias = (jnp.where(i > 0, 0, 1), 1)

    def far_logits(c, mps):
        kk = k_ref[pl.ds(pl.multiple_of(c * ck, ck), ck), :]
        new = []
        for hl in range(2):
            st = lax.dot_general(kk, qm[hl], _NT, preferred_element_type=_F32)
            mp = mps[hl]
            for u in range(ATT_CHUNK):
                tile = st[u * t:(u + 1) * t, :] + maskb_ref[c * ATT_CHUNK + u]
                lg_ref[hl, c * ATT_CHUNK + u] = tile
                mp = jnp.maximum(mp, _fold8(tile, "max"))
            new.append(mp)
        return tuple(new)

    mp0 = jnp.full((SUBLANES, t), NEG_BIG, _F32)
    mps = lax.fori_loop(0, n_far_chunks, far_logits, (mp0, mp0))
    kk = k_ref[near_rows, :]
    m = []
    for hl in range(2):
        st = lax.dot_general(kk, qm[hl], _NT, preferred_element_type=_F32)
        mp = mps[hl]
        for s in range(2):
            tile = st[s * t:(s + 1) * t, :] + maskn_ref[s] + tb_ref[hl, near_bias[s]]
            lgn_ref[hl, s] = tile
            mp = jnp.maximum(mp, _fold8(tile, "max"))
        m.append(jnp.max(mp, axis=0, keepdims=True))

    acc_ref[...] = jnp.zeros_like(acc_ref)

    def far_pv(c, lps):
        vv = v_ref[pl.ds(pl.multiple_of(c * ck, ck), ck), :]
        new = []
        for hl in range(2):
            pr = jnp.exp(lg_ref[hl, pl.ds(c * ATT_CHUNK, ATT_CHUNK)].reshape(ck, t) - m[hl])
            new.append(lps[hl] + _fold8(pr, "sum"))
            acc_ref[hl] += lax.dot_general(vv, pr.astype(_BF16), _TN, preferred_element_type=_F32)
        return tuple(new)

    lp0 = jnp.zeros((SUBLANES, t), _F32)
    lps = lax.fori_loop(0, n_far_chunks, far_pv, (lp0, lp0))
    vv = v_ref[near_rows, :]
    outs = []
    for hl in range(2):
        pr = jnp.exp(lgn_ref[hl].reshape(2 * t, t) - m[hl])
        lp = lps[hl] + _fold8(pr, "sum")
        o_t = acc_ref[hl] + lax.dot_general(vv, pr.astype(_BF16), _TN, preferred_element_type=_F32)
        o_t = o_t / jnp.sum(lp, axis=0, keepdims=True)
        outs.append(o_t[hl * HEAD_DIM:(hl + 1) * HEAD_DIM, :])
    o_ref[...] = jnp.concatenate(outs, axis=0).T.astype(o_ref.dtype)


def _rel_bucket_table(n_buckets):
    d = np.arange(REL_MAX_DIST, dtype=np.int32)
    max_exact = n_buckets // 2
    ratio = np.log(np.maximum(d, 1).astype(np.float32) / np.float32(max_exact)) / np.float32(
        math.log(REL_MAX_DIST / max_exact)) * np.float32(n_buckets - max_exact)
    large = np.minimum(max_exact + ratio.astype(np.int32), n_buckets - 1)
    return np.where(d < max_exact, d, large)


def _band_bias(rel_bias):
    n_buckets, n_heads = rel_bias.shape
    t = ATT_TILE
    assert REL_MAX_DIST <= t
    rb = rel_bias.astype(_F32)
    shifted = rb - rb[n_buckets - 1][None, :]
    by_dist = shifted[_rel_bucket_table(n_buckets)].T
    dv = jnp.concatenate([jnp.zeros((n_heads, t - 1), _F32), by_dist,
                          jnp.zeros((n_heads, 2 * t - REL_MAX_DIST + 1), _F32)], axis=1)
    slots = []
    for off in (t, 0):
        v = dv[:, off:off + 2 * t]
        flat = jnp.tile(v, (1, t))[:, t - 1:t - 1 + t * (2 * t - 1)]
        slots.append(flat.reshape(n_heads, t, 2 * t - 1)[:, :, :t])
    return jnp.stack(slots, axis=1)


def _dsa_attention(proj2, proj1, rel_bias, batch, seq, d_att, d_conv, n_idx_heads):
    t = ATT_TILE
    n_heads = d_att // HEAD_DIM
    n_pairs = n_heads // 2
    idx_w = n_idx_heads * IDX_DIM
    assert seq % t == 0 and n_heads % 2 == 0 and (3 * d_att) % idx_w == 0 and (2 * d_conv) % LANES == 0
    assert IDX_DIM + n_idx_heads <= LANES
    nq = seq // t
    n_sel = min(TOPK_MAX, seq // 4)
    assert n_sel <= t and seq & (seq - 1) == 0 and nq >= 2 and nq % ATT_CHUNK == 0
    cb = d_att // LANES
    kwb = 2 * d_conv // LANES
    tb = _band_bias(rel_bias)
    return pl.pallas_call(
        functools.partial(_dsa_kernel, n_idx_heads=n_idx_heads, n_sel=n_sel, n_pos=seq),
        grid=(batch, nq, n_pairs),
        in_specs=[pl.BlockSpec((t, LANES), lambda b, i, p: (b * nq + i, p)),
                  pl.BlockSpec((seq, LANES), lambda b, i, p: (b, cb + p)),
                  pl.BlockSpec((seq, LANES), lambda b, i, p: (b, 2 * cb + p)),
                  pl.BlockSpec((t, idx_w), lambda b, i, p: (b * nq + i, 3 * d_att // idx_w)),
                  pl.BlockSpec((seq, LANES), lambda b, i, p: (b, kwb)),
                  pl.BlockSpec((t, LANES), lambda b, i, p: (b * nq + i, kwb)),
                  pl.BlockSpec((2, 2, t, t), lambda b, i, p: (p, 0, 0, 0))],
        out_specs=pl.BlockSpec((t, LANES), lambda b, i, p: (b * nq + i, p)),
        out_shape=jax.ShapeDtypeStruct((batch * seq, d_att), _BF16),
        scratch_shapes=[pltpu.VMEM((seq, IDX_DIM), _BF16),
                        pltpu.VMEM((nq, t, t), jnp.int32),
                        pltpu.VMEM((nq, t, t), _F32),
                        pltpu.VMEM((2, t, t), _F32),
                        pltpu.VMEM((2, nq, t, t), _F32),
                        pltpu.VMEM((2, 2, t, t), _F32),
                        pltpu.VMEM((2, LANES, t), _F32),
                        pltpu.VMEM((1, t), jnp.int32)],
        compiler_params=_params("parallel", "arbitrary", "arbitrary"),
        name="dsa_attention",
    )(proj2, proj2, proj2, proj2, proj1, proj1, tb)


def _layer_norm(y, g, b):
    mu = jnp.mean(y, axis=-1, keepdims=True)
    d = y - mu
    var = jnp.mean(d * d, axis=-1, keepdims=True)
    return d * lax.rsqrt(var + LN_EPS) * g + b


def _outproj_kernel(c_ref, a_ref, w1_ref, w2_ref, b_ref, x_ref, g_ref, beta_ref, o_ref, ob_ref, ot_ref, *, alpha):
    mix = jnp.dot(c_ref[...], w1_ref[...], preferred_element_type=_F32)
    mix = mix + jnp.dot(a_ref[...], w2_ref[...], preferred_element_type=_F32) + b_ref[...]
    y = _layer_norm(alpha * x_ref[...] + mix, g_ref[...], beta_ref[...])
    o_ref[...] = y
    ob_ref[...] = y.astype(_BF16)
    ot_ref[...] = y.T.astype(_BF16)


def _out_projection(conv_out, att_out, w_out, b_out, x2, g, beta, alpha, tm):
    n, d = x2.shape
    dc = conv_out.shape[1]
    da = att_out.shape[1]
    vec = pl.BlockSpec((1, d), lambda i: (0, 0))
    return pl.pallas_call(
        functools.partial(_outproj_kernel, alpha=alpha),
        grid=(n // tm,),
        in_specs=[pl.BlockSpec((tm, dc), lambda i: (i, 0)),
                  pl.BlockSpec((tm, da), lambda i: (i, 0)),
                  pl.BlockSpec((dc, d), lambda i: (0, 0)),
                  pl.BlockSpec((da, d), lambda i: (0, 0)),
                  vec,
                  pl.BlockSpec((tm, d), lambda i: (i, 0)),
                  vec, vec],
        out_specs=[pl.BlockSpec((tm, d), lambda i: (i, 0)), pl.BlockSpec((tm, d), lambda i: (i, 0)),
                   pl.BlockSpec((d, tm), lambda i: (0, i))],
        out_shape=[jax.ShapeDtypeStruct((n, d), _F32), jax.ShapeDtypeStruct((n, d), _BF16),
                   jax.ShapeDtypeStruct((d, n), _BF16)],
        compiler_params=_params("parallel"),
        name="out_projection_ln1",
    )(conv_out, att_out, w_out[:dc], w_out[dc:], b_out.reshape(1, d), x2, g.reshape(1, d), beta.reshape(1, d))


def _peer_pairs(k):
    return [(i, j) for i in range(k) for j in range(k) if (i + 1) * (j + 1) <= k]


PEER_POPS = PEER_TOPK + 1


def _batcher_pairs(n):
    pairs = []
    p = 1
    while p < n:
        k = p
        while k >= 1:
            for j in range(k % p, n - k, 2 * k):
                for i in range(min(k, n - j - k)):
                    if (i + j) // (2 * p) == (i + j + k) // (2 * p):
                        pairs.append((i + j, i + j + k))
            k //= 2
        p *= 2
    return pairs


def _sort_desc(xs):
    xs = list(xs)
    for i, j in _batcher_pairs(len(xs)):
        xs[i], xs[j] = jnp.maximum(xs[i], xs[j]), jnp.minimum(xs[i], xs[j])
    return xs


def _bitonic_merge_desc(xs):
    xs = list(xs)
    n = len(xs)
    k = n // 2
    while k >= 1:
        for i in range(n):
            if i & k == 0:
                xs[i], xs[i + k] = jnp.maximum(xs[i], xs[i + k]), jnp.minimum(xs[i], xs[i + k])
        k //= 2
    return xs


def _partner(xs, shift):
    return [pltpu.roll(x, shift, 0) for x in xs]


def _merge_keep(a, b, dropped):
    n = len(a)
    hi = [jnp.maximum(a[i], b[n - 1 - i]) for i in range(n)]
    for i in range(n):
        dropped = jnp.maximum(dropped, jnp.minimum(a[i], b[n - 1 - i]))
    return _bitonic_merge_desc(hi), dropped


def _top_across_sublanes(xs, keep):
    dropped = jnp.full(xs[0].shape, -jnp.inf, _F32)
    for shift in (4, 2, 1):
        other = _partner(xs, shift)
        if 2 * len(xs) <= keep:
            xs = _bitonic_merge_desc(xs + other[::-1])
        else:
            assert len(xs) == keep
            xs, dropped = _merge_keep(xs, other, jnp.maximum(dropped, pltpu.roll(dropped, shift, 0)))
    return xs, dropped


def _pack_sublanes(slabs, sub):
    out = []
    for g in range(-(-len(slabs) // SUBLANES)):
        acc = jnp.full(slabs[0].shape, -jnp.inf, _F32)
        for s in range(SUBLANES):
            if g * SUBLANES + s < len(slabs):
                acc = jnp.where(sub == s, slabs[g * SUBLANES + s], acc)
        out.append(acc)
    return out


def _peer_select_kernel(x_ref, wq_ref, sk_ref, s2_ref, t_ref, e1_ref, e2_ref, *, n_heads, n_keys):
    tm = x_ref.shape[0]
    dk = sk_ref.shape[2]
    q = jnp.dot(x_ref[...], wq_ref[...], preferred_element_type=_F32).astype(_BF16)
    sub = lax.broadcasted_iota(jnp.int32, (SUBLANES, tm), 0)
    pairs = _peer_pairs(PEER_POPS)
    for h in range(n_heads):
        tops = []
        scores = []
        for c in range(2):
            g = 2 * h + c
            s_t = lax.dot_general(sk_ref[g], q[:, g * dk:(g + 1) * dk], _NT,
                                  preferred_element_type=_F32)
            scores.append(s_t)
            groups = _sort_desc([s_t[r * SUBLANES:(r + 1) * SUBLANES, :] for r in range(n_keys // SUBLANES)])
            top, nxt = _top_across_sublanes(groups, PEER_TOPK)
            tops.append(top + [nxt])
        a, b = tops
        cand = _pack_sublanes([a[i] + b[j] for (i, j) in pairs], sub)
        cand += [jnp.full((SUBLANES, tm), -jnp.inf, _F32)] * (SUBLANES - len(cand))
        top, nxt = _top_across_sublanes(_sort_desc(cand), PEER_TOPK)
        vals = [v[0:1, :] for v in top + [nxt]]
        a = [v[0:1, :] for v in a]
        b = [v[0:1, :] for v in b]
        z = jnp.zeros((1, tm), _F32)
        for v in vals[:PEER_TOPK]:
            z = z + jnp.exp(v - vals[0])
        theta = 0.5 * (vals[PEER_TOPK - 1] + vals[PEER_TOPK])
        s2_ref[h] = scores[1]
        t_ref[h] = theta - scores[0]
        e1_ref[h] = jnp.exp(scores[0] - a[0])
        e2_ref[h] = jnp.exp(scores[1] - b[0]) / z


def _peer_select(x1b, wq, sub_keys, tm):
    n, d = x1b.shape
    n_heads, _, n_keys, dk = sub_keys.shape
    assert n_keys == PEER_TOPK * SUBLANES
    sk = sub_keys.reshape(2 * n_heads, n_keys, dk).astype(_BF16)
    g2 = 2 * n_heads
    out = pl.BlockSpec((n_heads, n_keys, tm), lambda i: (0, 0, i))
    shape = jax.ShapeDtypeStruct((n_heads, n_keys, n), _F32)
    return pl.pallas_call(
        functools.partial(_peer_select_kernel, n_heads=n_heads, n_keys=n_keys),
        grid=(n // tm,),
        in_specs=[pl.BlockSpec((tm, d), lambda i: (i, 0)),
                  pl.BlockSpec((d, g2 * dk), lambda i: (0, 0)),
                  pl.BlockSpec((g2, n_keys, dk), lambda i: (0, 0, 0))],
        out_specs=[out, out, out, out],
        out_shape=[shape, shape, shape, shape],
        compiler_params=_params("parallel"),
        name="peer_select",
    )(x1b, wq, sk)


PEER_SUB = 256
PEER_UNIT = 256
PEER_PIECE = 512


def _peer_expert_kernel(xt_ref, u_ref, vt_ref, s2_ref, t_ref, e1_ref, e2_ref, x_ref, g_ref, beta_ref, o_ref,
                        y_ref, a0_ref, a1_ref, *, n_heads, n_keys, alpha, nk):
    k = pl.program_id(1)
    te = u_ref.shape[0]
    tm = xt_ref.shape[1]
    d = vt_ref.shape[0]
    kt = k
    units = [(r, c) for r in range(te // PEER_UNIT) for c in range(tm // PEER_SUB)]
    n_pieces = d // PEER_PIECE
    pieces_of = [[m for m in range(n_pieces) if m * len(units) // n_pieces == ui] for ui in range(len(units))]

    def body(a_cur, a_prev, first=True, second=True):
        def second_matmul_piece(m):
            rows = slice(m * PEER_PIECE, (m + 1) * PEER_PIECE)
            y_ref[rows, :] += jnp.dot(vt_ref[rows, :], a_prev[...], preferred_element_type=_F32)

        for ui, (r2, c2) in enumerate(units):
            if first:
                h_t = jnp.dot(u_ref[r2 * PEER_UNIT:(r2 + 1) * PEER_UNIT, :],
                              xt_ref[:, c2 * PEER_SUB:(c2 + 1) * PEER_SUB],
                              preferred_element_type=_F32)
            if second:
                for m in pieces_of[ui]:
                    second_matmul_piece(m)
            if not first:
                continue
            act = 0.5 * h_t * (1.0 + lax.erf(h_t * (2.0 ** -0.5)))
            for r in range(PEER_UNIT // n_keys):
                blk = r2 * (PEER_UNIT // n_keys) + r
                i1 = kt * (te // n_keys) + blk
                rows = slice(blk * n_keys, (blk + 1) * n_keys)
                for c in range(PEER_SUB // LANES):
                    cols = slice(c2 * PEER_SUB + c * LANES, c2 * PEER_SUB + (c + 1) * LANES)
                    gate = jnp.zeros((n_keys, LANES), _F32)
                    for h in range(n_heads):
                        sel = s2_ref[h, :, cols] >= t_ref[h, pl.ds(i1, 1), :][:, cols]
                        gate = gate + jnp.where(sel, e2_ref[h, :, cols] * e1_ref[h, pl.ds(i1, 1), :][:, cols],
                                                0.0)
                    a_cur[rows, cols] = (gate * act[r * n_keys:(r + 1) * n_keys,
                                                    c * LANES:(c + 1) * LANES]).astype(a_cur.dtype)

    bufs = (a0_ref, a1_ref)

    @pl.when(k == 0)
    def _():
        y_ref[...] = jnp.zeros_like(y_ref)
        body(bufs[0], bufs[1], second=False)

    for parity in range(2):
        @pl.when((k > 0) & (k < nk) & (k % 2 == parity))
        def _():
            body(bufs[parity], bufs[1 - parity])

    @pl.when(k == nk)
    def _():
        body(bufs[nk % 2], bufs[(nk - 1) % 2], first=False)
        y = alpha * x_ref[...] + y_ref[...].T
        o_ref[...] = _layer_norm(y, g_ref[...], beta_ref[...])


def _peer_experts(x1, x1t, u_b, v_tab, s2, t, e1, e2, g, beta, alpha, tm, te):
    n, d = x1.shape
    n_exp = u_b.shape[0]
    n_heads, n_keys, _ = s2.shape
    assert n_exp == n_keys * n_keys and te % PEER_UNIT == 0 and PEER_UNIT % n_keys == 0 and d % PEER_PIECE == 0
    assert n_exp % te == 0 and n % tm == 0 and tm % PEER_SUB == 0
    nk = n_exp // te
    vt_b = jnp.transpose(v_tab.reshape(nk, te, d), (0, 2, 1)).astype(_BF16)
    vec = pl.BlockSpec((1, d), lambda j, k: (0, 0))
    sel = pl.BlockSpec((n_heads, n_keys, tm), lambda j, k: (0, 0, j))
    once = dict(pipeline_mode=pl.Buffered(1))
    return pl.pallas_call(
        functools.partial(_peer_expert_kernel, n_heads=n_heads, n_keys=n_keys, alpha=alpha, nk=nk),
        grid=(n // tm, nk + 1),
        in_specs=[pl.BlockSpec((d, tm), lambda j, k: (0, j)),
                  pl.BlockSpec((te, d), lambda j, k: (jnp.minimum(k, nk - 1), 0)),
                  pl.BlockSpec((None, d, te), lambda j, k: (jnp.maximum(k - 1, 0), 0, 0)),
                  sel, sel, sel, sel,
                  pl.BlockSpec((tm, d), lambda j, k: (j, 0), **once),
                  vec, vec],
        out_specs=pl.BlockSpec((tm, d), lambda j, k: (j, 0), **once),
        out_shape=jax.ShapeDtypeStruct((n, d), _F32),
        scratch_shapes=[pltpu.VMEM((d, tm), _F32), pltpu.VMEM((te, tm), _BF16), pltpu.VMEM((te, tm), _BF16)],
        compiler_params=_params("parallel", "arbitrary"),
        name="peer_experts_ln2",
    )(x1t, u_b, vt_b, s2, t, e1, e2, x1, g.reshape(1, d), beta.reshape(1, d))


def _pick(n, pref):
    t = min(pref, n)
    while n % t:
        t //= 2
    return t


def kernel(x, w_in, w_out, b_out, dw_w, dw_b, conv_ln_g, conv_ln_b, rel_bias, ln1_g, ln1_b,
           peer_wq, peer_sub_keys, peer_u, peer_v, ln2_g, ln2_b):
    batch, seq, d = x.shape
    depth = w_in.shape[0]
    alpha = (2.0 * depth) ** 0.25
    d_conv = dw_w.shape[2]
    d_att = d - d_conv
    d_in = w_in.shape[2]
    n_idx_heads = (d_in - 2 * d_conv - 3 * d_att - IDX_DIM) // (IDX_DIM + 1)
    idx_w = n_idx_heads * IDX_DIM
    assert 2 * d_conv + 3 * d_att + idx_w + IDX_DIM + n_idx_heads == d_in
    n = batch * seq
    x2 = x.reshape(n, d)
    for l in range(depth):
        c0 = 2 * d_conv
        c1 = c0 + 3 * d_att + idx_w
        tail = jnp.pad(w_in[l][:, c1:], ((0, 0), (0, LANES - (d_in - c1))))
        w1 = jnp.concatenate([w_in[l][:, :c0], tail], axis=1).astype(_BF16)
        w2 = w_in[l][:, c0:c1].astype(_BF16)
        tm = _pick(n, 1024)
        proj1 = _matmul(x2, w1, _F32, tm, w1.shape[1])
        proj2 = _matmul(x2, w2, _BF16, tm, _pick(w2.shape[1], 1024))
        conv_out = _conv_group(proj1, dw_w[l], dw_b[l], conv_ln_g[l], conv_ln_b[l], batch, seq, d_conv,
                               _pick(seq, 256))
        att_out = _dsa_attention(proj2, proj1, rel_bias, batch, seq, d_att, d_conv, n_idx_heads)
        x1, x1b, x1t = _out_projection(conv_out, att_out, w_out[l].astype(_BF16), b_out[l], x2,
                                       ln1_g[l], ln1_b[l], alpha, _pick(n, 512))
        s2, t, e1, e2 = _peer_select(x1b, peer_wq[l].astype(_BF16), peer_sub_keys[l], _pick(n, 256))
        x2 = _peer_experts(x1, x1t, peer_u[l].astype(_BF16), peer_v[l], s2, t, e1, e2,
                           ln2_g[l], ln2_b[l], alpha, _pick(n, 512), 512)
    return x2.reshape(batch, seq, d)
```

```python
import functools
import math

import numpy as np
import jax
import jax.numpy as jnp
from jax import lax
from jax.experimental import pallas as pl
from jax.experimental.pallas import tpu as pltpu

HEAD_DIM = 64
IDX_DIM = 64
TOPK_MAX = 256
REL_MAX_DIST = 128
PEER_TOPK = 16
LN_EPS = 1e-5

LANES = 128
SUBLANES = 8
VMEM_LIMIT_BYTES = 56 * 1024 * 1024
NEG_BIG = -1e30
INT_MIN = -(2 ** 31)

_BF16 = jnp.bfloat16
_F32 = jnp.float32
_NT = (((1,), (1,)), ((), ()))
_TN = (((0,), (0,)), ((), ()))


def _params(*sem, flags=None):
    return pltpu.CompilerParams(dimension_semantics=sem, vmem_limit_bytes=VMEM_LIMIT_BYTES, flags=flags)


def _matmul_kernel(x_ref, w_ref, o_ref, xb_ref):
    @pl.when(pl.program_id(1) == 0)
    def _():
        xb_ref[...] = x_ref[...].astype(_BF16)

    o_ref[...] = jnp.dot(xb_ref[...], w_ref[...], preferred_element_type=_F32).astype(o_ref.dtype)


def _matmul(x, w, out_dtype, tm, tn):
    m, k = x.shape
    n = w.shape[1]
    assert m % tm == 0 and n % tn == 0
    return pl.pallas_call(
        _matmul_kernel,
        grid=(m // tm, n // tn),
        in_specs=[pl.BlockSpec((tm, k), lambda i, j: (i, 0)),
                  pl.BlockSpec((k, tn), lambda i, j: (0, j))],
        out_specs=pl.BlockSpec((tm, tn), lambda i, j: (i, j)),
        out_shape=jax.ShapeDtypeStruct((m, n), out_dtype),
        scratch_shapes=[pltpu.VMEM((tm, k), _BF16)],
        compiler_params=_params("parallel", "arbitrary"),
        name="proj_matmul",
    )(x, w)


CONV_HALO = 32
CONV_ROWS = 16
CONV_UNROLL = 4


def _conv_kernel(a1_ref, a2_ref, h1_ref, h2_ref, w_ref, b_ref, g_ref, beta_ref, o_ref, hs_ref, rot_ref,
                 *, t_rows, width):
    first = pl.program_id(1) == 0
    halo = h1_ref[...] * jax.nn.sigmoid(h2_ref[...])
    hs_ref[0:CONV_HALO, :] = jnp.where(first, 0.0, halo)
    hs_ref[CONV_HALO:CONV_HALO + t_rows, :] = a1_ref[...] * jax.nn.sigmoid(a2_ref[...])
    n_rot = t_rows + CONV_HALO - SUBLANES
    rot_ref[0, :, :] = hs_ref[...]
    for r in range(1, SUBLANES):
        rot_ref[r, 0:n_rot, :] = hs_ref[r:r + n_rot, :]
    base = CONV_HALO - (width - 1)
    inv_c = 1.0 / o_ref.shape[-1]

    def chunk(c, carry):
        row0 = pl.multiple_of(c * CONV_ROWS, CONV_ROWS)
        acc = jnp.broadcast_to(b_ref[...], (CONV_ROWS, o_ref.shape[-1]))
        for j in range(width):
            q, r = divmod(base + j, SUBLANES)
            tap = rot_ref[r, pl.ds(row0 + q * SUBLANES, CONV_ROWS), :]
            acc = acc + w_ref[j:j + 1, :] * tap
        mu = jnp.sum(acc, axis=-1, keepdims=True) * inv_c
        d = acc - mu
        var = jnp.sum(d * d, axis=-1, keepdims=True) * inv_c
        y = d * lax.rsqrt(var + LN_EPS) * g_ref[...] + beta_ref[...]
        o_ref[pl.ds(row0, CONV_ROWS), :] = (y * jax.nn.sigmoid(y)).astype(o_ref.dtype)
        return carry

    lax.fori_loop(0, t_rows // CONV_ROWS, chunk, 0, unroll=CONV_UNROLL)


def _conv_group(proj1, dw_w, dw_b, cn_g, cn_b, batch, seq, d_conv, t_rows):
    width = dw_w.shape[0]
    assert width - 1 <= CONV_HALO and seq % t_rows == 0 and t_rows % CONV_HALO == 0
    nt = seq // t_rows
    hb = t_rows // CONV_HALO

    def cur(col):
        return pl.BlockSpec((t_rows, d_conv), lambda b, s: (b * nt + s, col))

    def halo(col):
        return pl.BlockSpec((CONV_HALO, d_conv), lambda b, s: (jnp.maximum((b * nt + s) * hb - 1, 0), col))

    vec = pl.BlockSpec((1, d_conv), lambda b, s: (0, 0))
    return pl.pallas_call(
        functools.partial(_conv_kernel, t_rows=t_rows, width=width),
        grid=(batch, nt),
        in_specs=[cur(0), cur(1), halo(0), halo(1),
                  pl.BlockSpec((width, d_conv), lambda b, s: (0, 0)), vec, vec, vec],
        out_specs=pl.BlockSpec((t_rows, d_conv), lambda b, s: (b * nt + s, 0)),
        out_shape=jax.ShapeDtypeStruct((batch * seq, d_conv), _BF16),
        scratch_shapes=[pltpu.VMEM((t_rows + CONV_HALO, d_conv), _F32),
                        pltpu.VMEM((SUBLANES, t_rows + CONV_HALO, d_conv), _F32)],
        compiler_params=_params("parallel", "arbitrary"),
        name="conv_group",
    )(proj1, proj1, proj1, proj1, dw_w, dw_b.reshape(1, -1), cn_g.reshape(1, -1), cn_b.reshape(1, -1))


ATT_TILE = 256
ATT_CHUNK = 4


def _sortable_key(x):
    b = pltpu.bitcast(x, jnp.int32)
    return b ^ ((b >> 31) & jnp.int32(0x7FFFFFFF))


def _fold8(x, op):
    r, c = x.shape
    x = x.reshape(r // SUBLANES, SUBLANES, c)
    return jnp.sum(x, axis=0) if op == "sum" else jnp.max(x, axis=0)


def _dsa_kernel(q_ref, k_ref, v_ref, qi_ref, kw_ref, kwq_ref, tb_ref, o_ref,
                kib_ref, keys_ref, maskb_ref, maskn_ref, lg_ref, lgn_ref, acc_ref, cut_ref,
                *, n_idx_heads, n_sel, n_pos):
    t = ATT_TILE
    i = pl.program_id(1)
    p = pl.program_id(2)
    near_base = jnp.maximum(i - 1, 0)
    n_far = near_base
    n_far_chunks = lax.shift_right_logical(n_far + (ATT_CHUNK - 1), ATT_CHUNK.bit_length() - 1)
    row = lax.broadcasted_iota(jnp.int32, (t, t), 0)
    col = lax.broadcasted_iota(jnp.int32, (t, t), 1)

    def causal(kt):
        return (kt * t + row) <= (i * t + col)

    @pl.when(p == 0)
    def _select():
        @pl.when(i == 0)
        def _():
            kib_ref[...] = kw_ref[:, 0:IDX_DIM].astype(_BF16)

        w_t = kwq_ref[...].T

        def score_tile(kt, carry):
            ki = kib_ref[pl.ds(pl.multiple_of(kt * t, t), t), :]
            acc = jnp.zeros((t, t), _F32)
            for h in range(n_idx_heads):
                z = lax.dot_general(ki, qi_ref[:, h * IDX_DIM:(h + 1) * IDX_DIM], _NT,
                                    preferred_element_type=_F32)
                acc = acc + jnp.maximum(z, 0.0) * w_t[IDX_DIM + h:IDX_DIM + h + 1, :]
            acc = jnp.where(causal(kt), acc, -jnp.inf)
            keys_ref[kt] = _sortable_key(acc)
            return carry

        lax.fori_loop(0, i + 1, score_tile, 0)

        def count_ge(cand):
            def body(kt, pc):
                return pc + _fold8((keys_ref[kt] >= cand).astype(jnp.int32), "sum")
            pc = lax.fori_loop(0, i + 1, body, jnp.zeros((SUBLANES, t), jnp.int32))
            return jnp.sum(pc, axis=0, keepdims=True)

        zero = jnp.zeros((1, t), jnp.int32)
        tau0 = jnp.where(count_ge(zero) >= n_sel, zero, jnp.int32(INT_MIN))

        def bit_step(it, tau):
            cand = tau + (jnp.int32(1) << (30 - it))
            return jnp.where(count_ge(cand) >= n_sel, cand, tau)

        tau = lax.fori_loop(0, 31, bit_step, tau0)

        n_gt = count_ge(tau + 1)
        need = n_sel - n_gt
        excess = count_ge(tau) - n_gt - need
        cut_ref[...] = jnp.full((1, t), n_pos, jnp.int32)

        @pl.when(jnp.max(excess) > 0)
        def _():
            def count_eq_below(bound):
                def body(kt, pc):
                    hit = (keys_ref[kt] == tau) & ((kt * t + row) < bound)
                    return pc + _fold8(hit.astype(jnp.int32), "sum")
                pc = lax.fori_loop(0, i + 1, body, jnp.zeros((SUBLANES, t), jnp.int32))
                return jnp.sum(pc, axis=0, keepdims=True)

            def pos_step(it, pos):
                cand = pos + (jnp.int32(n_pos) >> (it + 1))
                return jnp.where(count_eq_below(cand) < need, cand, pos)

            cut_ref[...] = lax.fori_loop(0, n_pos.bit_length() - 1, pos_step, jnp.zeros((1, t), jnp.int32))

        cut = cut_ref[...]

        def mask_tile(kt, carry):
            key = keys_ref[kt]
            sel = ((key > tau) | ((key == tau) & ((kt * t + row) <= cut))) & causal(kt)
            maskb_ref[kt] = jnp.where(sel, 0.0, NEG_BIG)
            return carry

        lax.fori_loop(0, i + 1, mask_tile, 0)
        neg_tile = jnp.full((t, t), NEG_BIG, _F32)
        maskn_ref[0] = maskb_ref[near_base]
        maskn_ref[1] = jnp.where(i > 0, maskb_ref[i], neg_tile)

        def pad_tile(kt, carry):
            maskb_ref[kt] = neg_tile
            return carry

        lax.fori_loop(n_far, n_far_chunks * ATT_CHUNK, pad_tile, 0)

    ck = ATT_CHUNK * t
    lane = lax.broadcasted_iota(jnp.int32, (t, LANES), 1)
    q_all = q_ref[...] * jnp.asarray(HEAD_DIM ** -0.5, _BF16)
    qm = [jnp.where((lane >= hl * HEAD_DIM) & (lane < (hl + 1) * HEAD_DIM), q_all, jnp.zeros_like(q_all))
          for hl in range(2)]
    near_rows = pl.ds(pl.multiple_of(near_base * t, t), 2 * t)
    near_bias = (jnp.where(i > 0, 0, 1), 1)

    def far_logits(c, mps):
        kk = k_ref[pl.ds(pl.multiple_of(c * ck, ck), ck), :]
        new = []
        for hl in range(2):
            st = lax.dot_general(kk, qm[hl], _NT, preferred_element_type=_F32)
            mp = mps[hl]
            for u in range(ATT_CHUNK):
                tile = st[u * t:(u + 1) * t, :] + maskb_ref[c * ATT_CHUNK + u]
                lg_ref[hl, c * ATT_CHUNK + u] = tile
                mp = jnp.maximum(mp, _fold8(tile, "max"))
            new.append(mp)
        return tuple(new)

    mp0 = jnp.full((SUBLANES, t), NEG_BIG, _F32)
    mps = lax.fori_loop(0, n_far_chunks, far_logits, (mp0, mp0))
    kk = k_ref[near_rows, :]
    m = []
    for hl in range(2):
        st = lax.dot_general(kk, qm[hl], _NT, preferred_element_type=_F32)
        mp = mps[hl]
        for s in range(2):
            tile = st[s * t:(s + 1) * t, :] + maskn_ref[s] + tb_ref[hl, near_bias[s]]
            lgn_ref[hl, s] = tile
            mp = jnp.maximum(mp, _fold8(tile, "max"))
        m.append(jnp.max(mp, axis=0, keepdims=True))

    acc_ref[...] = jnp.zeros_like(acc_ref)

    def far_pv(c, lps):
        vv = v_ref[pl.ds(pl.multiple_of(c * ck, ck), ck), :]
        new = []
        for hl in range(2):
            pr = jnp.exp(lg_ref[hl, pl.ds(c * ATT_CHUNK, ATT_CHUNK)].reshape(ck, t) - m[hl])
            new.append(lps[hl] + _fold8(pr, "sum"))
            acc_ref[hl] += lax.dot_general(vv, pr.astype(_BF16), _TN, preferred_element_type=_F32)
        return tuple(new)

    lp0 = jnp.zeros((SUBLANES, t), _F32)
    lps = lax.fori_loop(0, n_far_chunks, far_pv, (lp0, lp0))
    vv = v_ref[near_rows, :]
    outs = []
    for hl in range(2):
        pr = jnp.exp(lgn_ref[hl].reshape(2 * t, t) - m[hl])
        lp = lps[hl] + _fold8(pr, "sum")
        o_t = acc_ref[hl] + lax.dot_general(vv, pr.astype(_BF16), _TN, preferred_element_type=_F32)
        o_t = o_t / jnp.sum(lp, axis=0, keepdims=True)
        outs.append(o_t[hl * HEAD_DIM:(hl + 1) * HEAD_DIM, :])
    o_ref[...] = jnp.concatenate(outs, axis=0).T.astype(o_ref.dtype)


def _rel_bucket_table(n_buckets):
    d = np.arange(REL_MAX_DIST, dtype=np.int32)
    max_exact = n_buckets // 2
    ratio = np.log(np.maximum(d, 1).astype(np.float32) / np.float32(max_exact)) / np.float32(
        math.log(REL_MAX_DIST / max_exact)) * np.float32(n_buckets - max_exact)
    large = np.minimum(max_exact + ratio.astype(np.int32), n_buckets - 1)
    return np.where(d < max_exact, d, large)


def _band_bias(rel_bias):
    n_buckets, n_heads = rel_bias.shape
    t = ATT_TILE
    assert REL_MAX_DIST <= t
    rb = rel_bias.astype(_F32)
    shifted = rb - rb[n_buckets - 1][None, :]
    by_dist = shifted[_rel_bucket_table(n_buckets)].T
    dv = jnp.concatenate([jnp.zeros((n_heads, t - 1), _F32), by_dist,
                          jnp.zeros((n_heads, 2 * t - REL_MAX_DIST + 1), _F32)], axis=1)
    slots = []
    for off in (t, 0):
        v = dv[:, off:off + 2 * t]
        flat = jnp.tile(v, (1, t))[:, t - 1:t - 1 + t * (2 * t - 1)]
        slots.append(flat.reshape(n_heads, t, 2 * t - 1)[:, :, :t])
    return jnp.stack(slots, axis=1)


def _dsa_attention(proj2, proj1, rel_bias, batch, seq, d_att, d_conv, n_idx_heads):
    t = ATT_TILE
    n_heads = d_att // HEAD_DIM
    n_pairs = n_heads // 2
    idx_w = n_idx_heads * IDX_DIM
    assert seq % t == 0 and n_heads % 2 == 0 and (3 * d_att) % idx_w == 0 and (2 * d_conv) % LANES == 0
    assert IDX_DIM + n_idx_heads <= LANES
    nq = seq // t
    n_sel = min(TOPK_MAX, seq // 4)
    assert n_sel <= t and seq & (seq - 1) == 0 and nq >= 2 and nq % ATT_CHUNK == 0
    cb = d_att // LANES
    kwb = 2 * d_conv // LANES
    tb = _band_bias(rel_bias)
    return pl.pallas_call(
        functools.partial(_dsa_kernel, n_idx_heads=n_idx_heads, n_sel=n_sel, n_pos=seq),
        grid=(batch, nq, n_pairs),
        in_specs=[pl.BlockSpec((t, LANES), lambda b, i, p: (b * nq + i, p)),
                  pl.BlockSpec((seq, LANES), lambda b, i, p: (b, cb + p)),
                  pl.BlockSpec((seq, LANES), lambda b, i, p: (b, 2 * cb + p)),
                  pl.BlockSpec((t, idx_w), lambda b, i, p: (b * nq + i, 3 * d_att // idx_w)),
                  pl.BlockSpec((seq, LANES), lambda b, i, p: (b, kwb)),
                  pl.BlockSpec((t, LANES), lambda b, i, p: (b * nq + i, kwb)),
                  pl.BlockSpec((2, 2, t, t), lambda b, i, p: (p, 0, 0, 0))],
        out_specs=pl.BlockSpec((t, LANES), lambda b, i, p: (b * nq + i, p)),
        out_shape=jax.ShapeDtypeStruct((batch * seq, d_att), _BF16),
        scratch_shapes=[pltpu.VMEM((seq, IDX_DIM), _BF16),
                        pltpu.VMEM((nq, t, t), jnp.int32),
                        pltpu.VMEM((nq, t, t), _F32),
                        pltpu.VMEM((2, t, t), _F32),
                        pltpu.VMEM((2, nq, t, t), _F32),
                        pltpu.VMEM((2, 2, t, t), _F32),
                        pltpu.VMEM((2, LANES, t), _F32),
                        pltpu.VMEM((1, t), jnp.int32)],
        compiler_params=_params("parallel", "arbitrary", "arbitrary"),
        name="dsa_attention",
    )(proj2, proj2, proj2, proj2, proj1, proj1, tb)


def _layer_norm(y, g, b):
    mu = jnp.mean(y, axis=-1, keepdims=True)
    d = y - mu
    var = jnp.mean(d * d, axis=-1, keepdims=True)
    return d * lax.rsqrt(var + LN_EPS) * g + b


def _outproj_kernel(c_ref, a_ref, w1_ref, w2_ref, b_ref, x_ref, g_ref, beta_ref, o_ref, ob_ref, ot_ref, *, alpha):
    mix = jnp.dot(c_ref[...], w1_ref[...], preferred_element_type=_F32)
    mix = mix + jnp.dot(a_ref[...], w2_ref[...], preferred_element_type=_F32) + b_ref[...]
    y = _layer_norm(alpha * x_ref[...] + mix, g_ref[...], beta_ref[...])
    o_ref[...] = y
    ob_ref[...] = y.astype(_BF16)
    ot_ref[...] = y.T.astype(_BF16)


def _out_projection(conv_out, att_out, w_out, b_out, x2, g, beta, alpha, tm):
    n, d = x2.shape
    dc = conv_out.shape[1]
    da = att_out.shape[1]
    vec = pl.BlockSpec((1, d), lambda i: (0, 0))
    return pl.pallas_call(
        functools.partial(_outproj_kernel, alpha=alpha),
        grid=(n // tm,),
        in_specs=[pl.BlockSpec((tm, dc), lambda i: (i, 0)),
                  pl.BlockSpec((tm, da), lambda i: (i, 0)),
                  pl.BlockSpec((dc, d), lambda i: (0, 0)),
                  pl.BlockSpec((da, d), lambda i: (0, 0)),
                  vec,
                  pl.BlockSpec((tm, d), lambda i: (i, 0)),
                  vec, vec],
        out_specs=[pl.BlockSpec((tm, d), lambda i: (i, 0)), pl.BlockSpec((tm, d), lambda i: (i, 0)),
                   pl.BlockSpec((d, tm), lambda i: (0, i))],
        out_shape=[jax.ShapeDtypeStruct((n, d), _F32), jax.ShapeDtypeStruct((n, d), _BF16),
                   jax.ShapeDtypeStruct((d, n), _BF16)],
        compiler_params=_params("parallel"),
        name="out_projection_ln1",
    )(conv_out, att_out, w_out[:dc], w_out[dc:], b_out.reshape(1, d), x2, g.reshape(1, d), beta.reshape(1, d))


def _peer_pairs(k):
    return [(i, j) for i in range(k) for j in range(k) if (i + 1) * (j + 1) <= k]


PEER_POPS = PEER_TOPK + 1


def _batcher_pairs(n):
    pairs = []
    p = 1
    while p < n:
        k = p
        while k >= 1:
            for j in range(k % p, n - k, 2 * k):
                for i in range(min(k, n - j - k)):
                    if (i + j) // (2 * p) == (i + j + k) // (2 * p):
                        pairs.append((i + j, i + j + k))
            k //= 2
        p *= 2
    return pairs


def _sort_desc(xs):
    xs = list(xs)
    for i, j in _batcher_pairs(len(xs)):
        xs[i], xs[j] = jnp.maximum(xs[i], xs[j]), jnp.minimum(xs[i], xs[j])
    return xs


def _bitonic_merge_desc(xs):
    xs = list(xs)
    n = len(xs)
    k = n // 2
    while k >= 1:
        for i in range(n):
            if i & k == 0:
                xs[i], xs[i + k] = jnp.maximum(xs[i], xs[i + k]), jnp.minimum(xs[i], xs[i + k])
        k //= 2
    return xs


def _partner(xs, shift):
    return [pltpu.roll(x, shift, 0) for x in xs]


def _merge_keep(a, b, dropped):
    n = len(a)
    hi = [jnp.maximum(a[i], b[n - 1 - i]) for i in range(n)]
    for i in range(n):
        dropped = jnp.maximum(dropped, jnp.minimum(a[i], b[n - 1 - i]))
    return _bitonic_merge_desc(hi), dropped


def _top_across_sublanes(xs, keep):
    dropped = jnp.full(xs[0].shape, -jnp.inf, _F32)
    for shift in (4, 2, 1):
        other = _partner(xs, shift)
        if 2 * len(xs) <= keep:
            xs = _bitonic_merge_desc(xs + other[::-1])
        else:
            assert len(xs) == keep
            xs, dropped = _merge_keep(xs, other, jnp.maximum(dropped, pltpu.roll(dropped, shift, 0)))
    return xs, dropped


def _pack_sublanes(slabs, sub):
    out = []
    for g in range(-(-len(slabs) // SUBLANES)):
        acc = jnp.full(slabs[0].shape, -jnp.inf, _F32)
        for s in range(SUBLANES):
            if g * SUBLANES + s < len(slabs):
                acc = jnp.where(sub == s, slabs[g * SUBLANES + s], acc)
        out.append(acc)
    return out


def _peer_select_kernel(x_ref, wq_ref, sk_ref, s2_ref, t_ref, e1_ref, e2_ref, *, n_heads, n_keys):
    tm = x_ref.shape[0]
    dk = sk_ref.shape[2]
    q = jnp.dot(x_ref[...], wq_ref[...], preferred_element_type=_F32).astype(_BF16)
    sub = lax.broadcasted_iota(jnp.int32, (SUBLANES, tm), 0)
    pairs = _peer_pairs(PEER_POPS)
    for h in range(n_heads):
        tops = []
        scores = []
        for c in range(2):
            g = 2 * h + c
            s_t = lax.dot_general(sk_ref[g], q[:, g * dk:(g + 1) * dk], _NT,
                                  preferred_element_type=_F32)
            scores.append(s_t)
            groups = _sort_desc([s_t[r * SUBLANES:(r + 1) * SUBLANES, :] for r in range(n_keys // SUBLANES)])
            top, nxt = _top_across_sublanes(groups, PEER_TOPK)
            tops.append(top + [nxt])
        a, b = tops
        cand = _pack_sublanes([a[i] + b[j] for (i, j) in pairs], sub)
        cand += [jnp.full((SUBLANES, tm), -jnp.inf, _F32)] * (SUBLANES - len(cand))
        top, nxt = _top_across_sublanes(_sort_desc(cand), PEER_TOPK)
        vals = [v[0:1, :] for v in top + [nxt]]
        a = [v[0:1, :] for v in a]
        b = [v[0:1, :] for v in b]
        z = jnp.zeros((1, tm), _F32)
        for v in vals[:PEER_TOPK]:
            z = z + jnp.exp(v - vals[0])
        theta = 0.5 * (vals[PEER_TOPK - 1] + vals[PEER_TOPK])
        s2_ref[h] = scores[1]
        t_ref[h] = theta - scores[0]
        e1_ref[h] = jnp.exp(scores[0] - a[0])
        e2_ref[h] = jnp.exp(scores[1] - b[0]) / z


def _peer_select(x1b, wq, sub_keys, tm):
    n, d = x1b.shape
    n_heads, _, n_keys, dk = sub_keys.shape
    assert n_keys == PEER_TOPK * SUBLANES
    sk = sub_keys.reshape(2 * n_heads, n_keys, dk).astype(_BF16)
    g2 = 2 * n_heads
    out = pl.BlockSpec((n_heads, n_keys, tm), lambda i: (0, 0, i))
    shape = jax.ShapeDtypeStruct((n_heads, n_keys, n), _F32)
    return pl.pallas_call(
        functools.partial(_peer_select_kernel, n_heads=n_heads, n_keys=n_keys),
        grid=(n // tm,),
        in_specs=[pl.BlockSpec((tm, d), lambda i: (i, 0)),
                  pl.BlockSpec((d, g2 * dk), lambda i: (0, 0)),
                  pl.BlockSpec((g2, n_keys, dk), lambda i: (0, 0, 0))],
        out_specs=[out, out, out, out],
        out_shape=[shape, shape, shape, shape],
        compiler_params=_params("parallel"),
        name="peer_select",
    )(x1b, wq, sk)


PEER_SUB = 256
PEER_UNIT = 256
PEER_PIECE = 512


def _peer_expert_kernel(xt_ref, u_ref, vt_ref, s2_ref, t_ref, e1_ref, e2_ref, x_ref, g_ref, beta_ref, o_ref,
                        y_ref, a0_ref, a1_ref, *, n_heads, n_keys, alpha, nk):
    k = pl.program_id(1)
    te = u_ref.shape[0]
    tm = xt_ref.shape[1]
    d = vt_ref.shape[0]
    kt = k
    units = [(r, c) for r in range(te // PEER_UNIT) for c in range(tm // PEER_SUB)]
    n_pieces = d // PEER_PIECE
    pieces_of = [[m for m in range(n_pieces) if m * len(units) // n_pieces == ui] for ui in range(len(units))]

    def body(a_cur, a_prev, first=True, second=True):
        def second_matmul_piece(m):
            rows = slice(m * PEER_PIECE, (m + 1) * PEER_PIECE)
            y_ref[rows, :] += jnp.dot(vt_ref[rows, :], a_prev[...], preferred_element_type=_F32)

        for ui, (r2, c2) in enumerate(units):
            if first:
                h_t = jnp.dot(u_ref[r2 * PEER_UNIT:(r2 + 1) * PEER_UNIT, :],
                              xt_ref[:, c2 * PEER_SUB:(c2 + 1) * PEER_SUB],
                              preferred_element_type=_F32)
            if second:
                for m in pieces_of[ui]:
                    second_matmul_piece(m)
            if not first:
                continue
            act = 0.5 * h_t * (1.0 + lax.erf(h_t * (2.0 ** -0.5)))
            for r in range(PEER_UNIT // n_keys):
                blk = r2 * (PEER_UNIT // n_keys) + r
                i1 = kt * (te // n_keys) + blk
                rows = slice(blk * n_keys, (blk + 1) * n_keys)
                for c in range(PEER_SUB // LANES):
                    cols = slice(c2 * PEER_SUB + c * LANES, c2 * PEER_SUB + (c + 1) * LANES)
                    gate = jnp.zeros((n_keys, LANES), _F32)
                    for h in range(n_heads):
                        sel = s2_ref[h, :, cols] >= t_ref[h, pl.ds(i1, 1), :][:, cols]
                        gate = gate + jnp.where(sel, e2_ref[h, :, cols] * e1_ref[h, pl.ds(i1, 1), :][:, cols],
                                                0.0)
                    a_cur[rows, cols] = (gate * act[r * n_keys:(r + 1) * n_keys,
                                                    c * LANES:(c + 1) * LANES]).astype(a_cur.dtype)

    bufs = (a0_ref, a1_ref)

    @pl.when(k == 0)
    def _():
        y_ref[...] = jnp.zeros_like(y_ref)
        body(bufs[0], bufs[1], second=False)

    for parity in range(2):
        @pl.when((k > 0) & (k < nk) & (k % 2 == parity))
        def _():
            body(bufs[parity], bufs[1 - parity])

    @pl.when(k == nk)
    def _():
        body(bufs[nk % 2], bufs[(nk - 1) % 2], first=False)
        y = alpha * x_ref[...] + y_ref[...].T
        o_ref[...] = _layer_norm(y, g_ref[...], beta_ref[...])


def _peer_experts(x1, x1t, u_b, v_tab, s2, t, e1, e2, g, beta, alpha, tm, te):
    n, d = x1.shape
    n_exp = u_b.shape[0]
    n_heads, n_keys, _ = s2.shape
    assert n_exp == n_keys * n_keys and te % PEER_UNIT == 0 and PEER_UNIT % n_keys == 0 and d % PEER_PIECE == 0
    assert n_exp % te == 0 and n % tm == 0 and tm % PEER_SUB == 0
    nk = n_exp // te
    vt_b = jnp.transpose(v_tab.reshape(nk, te, d), (0, 2, 1)).astype(_BF16)
    vec = pl.BlockSpec((1, d), lambda j, k: (0, 0))
    once = dict(pipeline_mode=pl.Buffered(1))
    sel = pl.BlockSpec((n_heads, n_keys, tm), lambda j, k: (0, 0, j), **once)
    return pl.pallas_call(
        functools.partial(_peer_expert_kernel, n_heads=n_heads, n_keys=n_keys, alpha=alpha, nk=nk),
        grid=(n // tm, nk + 1),
        in_specs=[pl.BlockSpec((d, tm), lambda j, k: (0, j)),
                  pl.BlockSpec((te, d), lambda j, k: (jnp.minimum(k, nk - 1), 0)),
                  pl.BlockSpec((None, d, te), lambda j, k: (jnp.maximum(k - 1, 0), 0, 0)),
                  sel, sel, sel, sel,
                  pl.BlockSpec((tm, d), lambda j, k: (j, 0), **once),
                  vec, vec],
        out_specs=pl.BlockSpec((tm, d), lambda j, k: (j, 0), **once),
        out_shape=jax.ShapeDtypeStruct((n, d), _F32),
        scratch_shapes=[pltpu.VMEM((d, tm), _F32), pltpu.VMEM((te, tm), _BF16), pltpu.VMEM((te, tm), _BF16)],
        compiler_params=_params("parallel", "arbitrary"),
        name="peer_experts_ln2",
    )(x1t, u_b, vt_b, s2, t, e1, e2, x1, g.reshape(1, d), beta.reshape(1, d))


def _pick(n, pref):
    t = min(pref, n)
    while n % t:
        t //= 2
    return t


def kernel(x, w_in, w_out, b_out, dw_w, dw_b, conv_ln_g, conv_ln_b, rel_bias, ln1_g, ln1_b,
           peer_wq, peer_sub_keys, peer_u, peer_v, ln2_g, ln2_b):
    batch, seq, d = x.shape
    depth = w_in.shape[0]
    alpha = (2.0 * depth) ** 0.25
    d_conv = dw_w.shape[2]
    d_att = d - d_conv
    d_in = w_in.shape[2]
    n_idx_heads = (d_in - 2 * d_conv - 3 * d_att - IDX_DIM) // (IDX_DIM + 1)
    idx_w = n_idx_heads * IDX_DIM
    assert 2 * d_conv + 3 * d_att + idx_w + IDX_DIM + n_idx_heads == d_in
    n = batch * seq
    x2 = x.reshape(n, d)
    for l in range(depth):
        c0 = 2 * d_conv
        c1 = c0 + 3 * d_att + idx_w
        tail = jnp.pad(w_in[l][:, c1:], ((0, 0), (0, LANES - (d_in - c1))))
        w1 = jnp.concatenate([w_in[l][:, :c0], tail], axis=1).astype(_BF16)
        w2 = w_in[l][:, c0:c1].astype(_BF16)
        tm = _pick(n, 1024)
        proj1 = _matmul(x2, w1, _F32, tm, w1.shape[1])
        proj2 = _matmul(x2, w2, _BF16, tm, _pick(w2.shape[1], 1024))
        conv_out = _conv_group(proj1, dw_w[l], dw_b[l], conv_ln_g[l], conv_ln_b[l], batch, seq, d_conv,
                               _pick(seq, 256))
        att_out = _dsa_attention(proj2, proj1, rel_bias, batch, seq, d_att, d_conv, n_idx_heads)
        x1, x1b, x1t = _out_projection(conv_out, att_out, w_out[l].astype(_BF16), b_out[l], x2,
                                       ln1_g[l], ln1_b[l], alpha, _pick(n, 512))
        s2, t, e1, e2 = _peer_select(x1b, peer_wq[l].astype(_BF16), peer_sub_keys[l], _pick(n, 256))
        x2 = _peer_experts(x1, x1t, peer_u[l].astype(_BF16), peer_v[l], s2, t, e1, e2,
                           ln2_g[l], ln2_b[l], alpha, _pick(n, 512), 1024)
    return x2.reshape(batch, seq, d)
```

```python
import functools
import math

import numpy as np
import jax
import jax.numpy as jnp
from jax import lax
from jax.experimental import pallas as pl
from jax.experimental.pallas import tpu as pltpu

HEAD_DIM = 64
IDX_DIM = 64
TOPK_MAX = 256
REL_MAX_DIST = 128
PEER_TOPK = 16
LN_EPS = 1e-5

LANES = 128
SUBLANES = 8
VMEM_LIMIT_BYTES = 56 * 1024 * 1024
NEG_BIG = -1e30
INT_MIN = -(2 ** 31)

_BF16 = jnp.bfloat16
_F32 = jnp.float32
_NT = (((1,), (1,)), ((), ()))
_TN = (((0,), (0,)), ((), ()))


def _params(*sem, flags=None):
    return pltpu.CompilerParams(dimension_semantics=sem, vmem_limit_bytes=VMEM_LIMIT_BYTES, flags=flags)


def _matmul_kernel(x_ref, w_ref, o_ref, xb_ref):
    @pl.when(pl.program_id(1) == 0)
    def _():
        xb_ref[...] = x_ref[...].astype(_BF16)

    o_ref[...] = jnp.dot(xb_ref[...], w_ref[...], preferred_element_type=_F32).astype(o_ref.dtype)


def _matmul(x, w, out_dtype, tm, tn):
    m, k = x.shape
    n = w.shape[1]
    assert m % tm == 0 and n % tn == 0
    return pl.pallas_call(
        _matmul_kernel,
        grid=(m // tm, n // tn),
        in_specs=[pl.BlockSpec((tm, k), lambda i, j: (i, 0)),
                  pl.BlockSpec((k, tn), lambda i, j: (0, j))],
        out_specs=pl.BlockSpec((tm, tn), lambda i, j: (i, j)),
        out_shape=jax.ShapeDtypeStruct((m, n), out_dtype),
        scratch_shapes=[pltpu.VMEM((tm, k), _BF16)],
        compiler_params=_params("parallel", "arbitrary"),
        name="proj_matmul",
    )(x, w)


CONV_HALO = 32
CONV_ROWS = 16
CONV_UNROLL = 4


def _conv_kernel(a1_ref, a2_ref, h1_ref, h2_ref, w_ref, b_ref, g_ref, beta_ref, o_ref, hs_ref, rot_ref,
                 *, t_rows, width):
    first = pl.program_id(1) == 0
    halo = h1_ref[...] * jax.nn.sigmoid(h2_ref[...])
    hs_ref[0:CONV_HALO, :] = jnp.where(first, 0.0, halo)
    hs_ref[CONV_HALO:CONV_HALO + t_rows, :] = a1_ref[...] * jax.nn.sigmoid(a2_ref[...])
    n_rot = t_rows + CONV_HALO - SUBLANES
    rot_ref[0, :, :] = hs_ref[...]
    for r in range(1, SUBLANES):
        rot_ref[r, 0:n_rot, :] = hs_ref[r:r + n_rot, :]
    base = CONV_HALO - (width - 1)
    inv_c = 1.0 / o_ref.shape[-1]

    def chunk(c, carry):
        row0 = pl.multiple_of(c * CONV_ROWS, CONV_ROWS)
        acc = jnp.broadcast_to(b_ref[...], (CONV_ROWS, o_ref.shape[-1]))
        for j in range(width):
            q, r = divmod(base + j, SUBLANES)
            tap = rot_ref[r, pl.ds(row0 + q * SUBLANES, CONV_ROWS), :]
            acc = acc + w_ref[j:j + 1, :] * tap
        mu = jnp.sum(acc, axis=-1, keepdims=True) * inv_c
        d = acc - mu
        var = jnp.sum(d * d, axis=-1, keepdims=True) * inv_c
        y = d * lax.rsqrt(var + LN_EPS) * g_ref[...] + beta_ref[...]
        o_ref[pl.ds(row0, CONV_ROWS), :] = (y * jax.nn.sigmoid(y)).astype(o_ref.dtype)
        return carry

    lax.fori_loop(0, t_rows // CONV_ROWS, chunk, 0, unroll=CONV_UNROLL)


def _conv_group(proj1, dw_w, dw_b, cn_g, cn_b, batch, seq, d_conv, t_rows):
    width = dw_w.shape[0]
    assert width - 1 <= CONV_HALO and seq % t_rows == 0 and t_rows % CONV_HALO == 0
    nt = seq // t_rows
    hb = t_rows // CONV_HALO

    def cur(col):
        return pl.BlockSpec((t_rows, d_conv), lambda b, s: (b * nt + s, col))

    def halo(col):
        return pl.BlockSpec((CONV_HALO, d_conv), lambda b, s: (jnp.maximum((b * nt + s) * hb - 1, 0), col))

    vec = pl.BlockSpec((1, d_conv), lambda b, s: (0, 0))
    return pl.pallas_call(
        functools.partial(_conv_kernel, t_rows=t_rows, width=width),
        grid=(batch, nt),
        in_specs=[cur(0), cur(1), halo(0), halo(1),
                  pl.BlockSpec((width, d_conv), lambda b, s: (0, 0)), vec, vec, vec],
        out_specs=pl.BlockSpec((t_rows, d_conv), lambda b, s: (b * nt + s, 0)),
        out_shape=jax.ShapeDtypeStruct((batch * seq, d_conv), _BF16),
        scratch_shapes=[pltpu.VMEM((t_rows + CONV_HALO, d_conv), _F32),
                        pltpu.VMEM((SUBLANES, t_rows + CONV_HALO, d_conv), _F32)],
        compiler_params=_params("parallel", "arbitrary"),
        name="conv_group",
    )(proj1, proj1, proj1, proj1, dw_w, dw_b.reshape(1, -1), cn_g.reshape(1, -1), cn_b.reshape(1, -1))


ATT_TILE = 256
ATT_CHUNK = 4


def _sortable_key(x):
    b = pltpu.bitcast(x, jnp.int32)
    return b ^ ((b >> 31) & jnp.int32(0x7FFFFFFF))


def _fold8(x, op):
    r, c = x.shape
    x = x.reshape(r // SUBLANES, SUBLANES, c)
    return jnp.sum(x, axis=0) if op == "sum" else jnp.max(x, axis=0)


def _dsa_kernel(q_ref, k_ref, v_ref, qi_ref, kw_ref, kwq_ref, tb_ref, o_ref,
                kib_ref, keys_ref, maskb_ref, maskn_ref, lg_ref, lgn_ref, acc_ref, cut_ref,
                *, n_idx_heads, n_sel, n_pos):
    t = ATT_TILE
    i = pl.program_id(1)
    p = pl.program_id(2)
    near_base = jnp.maximum(i - 1, 0)
    n_far = near_base
    n_far_chunks = lax.shift_right_logical(n_far + (ATT_CHUNK - 1), ATT_CHUNK.bit_length() - 1)
    row = lax.broadcasted_iota(jnp.int32, (t, t), 0)
    col = lax.broadcasted_iota(jnp.int32, (t, t), 1)

    def causal(kt):
        return (kt * t + row) <= (i * t + col)

    @pl.when(p == 0)
    def _select():
        @pl.when(i == 0)
        def _():
            kib_ref[...] = kw_ref[:, 0:IDX_DIM].astype(_BF16)

        w_t = kwq_ref[...].T

        def score_tile(kt, carry):
            ki = kib_ref[pl.ds(pl.multiple_of(kt * t, t), t), :]
            acc = jnp.zeros((t, t), _F32)
            for h in range(n_idx_heads):
                z = lax.dot_general(ki, qi_ref[:, h * IDX_DIM:(h + 1) * IDX_DIM], _NT,
                                    preferred_element_type=_F32)
                acc = acc + jnp.maximum(z, 0.0) * w_t[IDX_DIM + h:IDX_DIM + h + 1, :]
            acc = jnp.where(causal(kt), acc, -jnp.inf)
            keys_ref[kt] = _sortable_key(acc)
            return carry

        lax.fori_loop(0, i + 1, score_tile, 0)

        def count_ge(cand):
            def body(kt, pc):
                return pc + _fold8((keys_ref[kt] >= cand).astype(jnp.int32), "sum")
            pc = lax.fori_loop(0, i + 1, body, jnp.zeros((SUBLANES, t), jnp.int32))
            return jnp.sum(pc, axis=0, keepdims=True)

        zero = jnp.zeros((1, t), jnp.int32)
        tau0 = jnp.where(count_ge(zero) >= n_sel, zero, jnp.int32(INT_MIN))

        def bit_step(it, tau):
            cand = tau + (jnp.int32(1) << (30 - it))
            return jnp.where(count_ge(cand) >= n_sel, cand, tau)

        tau = lax.fori_loop(0, 31, bit_step, tau0)

        n_gt = count_ge(tau + 1)
        need = n_sel - n_gt
        excess = count_ge(tau) - n_gt - need
        cut_ref[...] = jnp.full((1, t), n_pos, jnp.int32)

        @pl.when(jnp.max(excess) > 0)
        def _():
            def count_eq_below(bound):
                def body(kt, pc):
                    hit = (keys_ref[kt] == tau) & ((kt * t + row) < bound)
                    return pc + _fold8(hit.astype(jnp.int32), "sum")
                pc = lax.fori_loop(0, i + 1, body, jnp.zeros((SUBLANES, t), jnp.int32))
                return jnp.sum(pc, axis=0, keepdims=True)

            def pos_step(it, pos):
                cand = pos + (jnp.int32(n_pos) >> (it + 1))
                return jnp.where(count_eq_below(cand) < need, cand, pos)

            cut_ref[...] = lax.fori_loop(0, n_pos.bit_length() - 1, pos_step, jnp.zeros((1, t), jnp.int32))

        cut = cut_ref[...]

        def mask_tile(kt, carry):
            key = keys_ref[kt]
            sel = ((key > tau) | ((key == tau) & ((kt * t + row) <= cut))) & causal(kt)
            maskb_ref[kt] = jnp.where(sel, 0.0, NEG_BIG)
            return carry

        lax.fori_loop(0, i + 1, mask_tile, 0)
        neg_tile = jnp.full((t, t), NEG_BIG, _F32)
        maskn_ref[0] = maskb_ref[near_base]
        maskn_ref[1] = jnp.where(i > 0, maskb_ref[i], neg_tile)

        def pad_tile(kt, carry):
            maskb_ref[kt] = neg_tile
            return carry

        lax.fori_loop(n_far, n_far_chunks * ATT_CHUNK, pad_tile, 0)

    ck = ATT_CHUNK * t
    lane = lax.broadcasted_iota(jnp.int32, (t, LANES), 1)
    q_all = q_ref[...] * jnp.asarray(HEAD_DIM ** -0.5, _BF16)
    qm = [jnp.where((lane >= hl * HEAD_DIM) & (lane < (hl + 1) * HEAD_DIM), q_all, jnp.zeros_like(q_all))
          for hl in range(2)]
    near_rows = pl.ds(pl.multiple_of(near_base * t, t), 2 * t)
    near_bias = (jnp.where(i > 0, 0, 1), 1)

    def far_logits(c, mps):
        kk = k_ref[pl.ds(pl.multiple_of(c * ck, ck), ck), :]
        new = []
        for hl in range(2):
            st = lax.dot_general(kk, qm[hl], _NT, preferred_element_type=_F32)
            mp = mps[hl]
            for u in range(ATT_CHUNK):
                tile = st[u * t:(u + 1) * t, :] + maskb_ref[c * ATT_CHUNK + u]
                lg_ref[hl, c * ATT_CHUNK + u] = tile
                mp = jnp.maximum(mp, _fold8(tile, "max"))
            new.append(mp)
        return tuple(new)

    mp0 = jnp.full((SUBLANES, t), NEG_BIG, _F32)
    mps = lax.fori_loop(0, n_far_chunks, far_logits, (mp0, mp0))
    kk = k_ref[near_rows, :]
    m = []
    for hl in range(2):
        st = lax.dot_general(kk, qm[hl], _NT, preferred_element_type=_F32)
        mp = mps[hl]
        for s in range(2):
            tile = st[s * t:(s + 1) * t, :] + maskn_ref[s] + tb_ref[hl, near_bias[s]]
            lgn_ref[hl, s] = tile
            mp = jnp.maximum(mp, _fold8(tile, "max"))
        m.append(jnp.max(mp, axis=0, keepdims=True))

    acc_ref[...] = jnp.zeros_like(acc_ref)

    def far_pv(c, lps):
        vv = v_ref[pl.ds(pl.multiple_of(c * ck, ck), ck), :]
        new = []
        for hl in range(2):
            pr = jnp.exp(lg_ref[hl, pl.ds(c * ATT_CHUNK, ATT_CHUNK)].reshape(ck, t) - m[hl])
            new.append(lps[hl] + _fold8(pr, "sum"))
            acc_ref[hl] += lax.dot_general(vv, pr.astype(_BF16), _TN, preferred_element_type=_F32)
        return tuple(new)

    lp0 = jnp.zeros((SUBLANES, t), _F32)
    lps = lax.fori_loop(0, n_far_chunks, far_pv, (lp0, lp0))
    vv = v_ref[near_rows, :]
    outs = []
    for hl in range(2):
        pr = jnp.exp(lgn_ref[hl].reshape(2 * t, t) - m[hl])
        lp = lps[hl] + _fold8(pr, "sum")
        o_t = acc_ref[hl] + lax.dot_general(vv, pr.astype(_BF16), _TN, preferred_element_type=_F32)
        o_t = o_t / jnp.sum(lp, axis=0, keepdims=True)
        outs.append(o_t[hl * HEAD_DIM:(hl + 1) * HEAD_DIM, :])
    o_ref[...] = jnp.concatenate(outs, axis=0).T.astype(o_ref.dtype)


def _rel_bucket_table(n_buckets):
    d = np.arange(REL_MAX_DIST, dtype=np.int32)
    max_exact = n_buckets // 2
    ratio = np.log(np.maximum(d, 1).astype(np.float32) / np.float32(max_exact)) / np.float32(
        math.log(REL_MAX_DIST / max_exact)) * np.float32(n_buckets - max_exact)
    large = np.minimum(max_exact + ratio.astype(np.int32), n_buckets - 1)
    return np.where(d < max_exact, d, large)


def _band_bias(rel_bias):
    n_buckets, n_heads = rel_bias.shape
    t = ATT_TILE
    assert REL_MAX_DIST <= t
    rb = rel_bias.astype(_F32)
    shifted = rb - rb[n_buckets - 1][None, :]
    by_dist = shifted[_rel_bucket_table(n_buckets)].T
    dv = jnp.concatenate([jnp.zeros((n_heads, t - 1), _F32), by_dist,
                          jnp.zeros((n_heads, 2 * t - REL_MAX_DIST + 1), _F32)], axis=1)
    slots = []
    for off in (t, 0):
        v = dv[:, off:off + 2 * t]
        flat = jnp.tile(v, (1, t))[:, t - 1:t - 1 + t * (2 * t - 1)]
        slots.append(flat.reshape(n_heads, t, 2 * t - 1)[:, :, :t])
    return jnp.stack(slots, axis=1)


def _dsa_attention(proj2, proj1, rel_bias, batch, seq, d_att, d_conv, n_idx_heads):
    t = ATT_TILE
    n_heads = d_att // HEAD_DIM
    n_pairs = n_heads // 2
    idx_w = n_idx_heads * IDX_DIM
    assert seq % t == 0 and n_heads % 2 == 0 and (3 * d_att) % idx_w == 0 and (2 * d_conv) % LANES == 0
    assert IDX_DIM + n_idx_heads <= LANES
    nq = seq // t
    n_sel = min(TOPK_MAX, seq // 4)
    assert n_sel <= t and seq & (seq - 1) == 0 and nq >= 2 and nq % ATT_CHUNK == 0
    cb = d_att // LANES
    kwb = 2 * d_conv // LANES
    tb = _band_bias(rel_bias)
    return pl.pallas_call(
        functools.partial(_dsa_kernel, n_idx_heads=n_idx_heads, n_sel=n_sel, n_pos=seq),
        grid=(batch, nq, n_pairs),
        in_specs=[pl.BlockSpec((t, LANES), lambda b, i, p: (b * nq + i, p)),
                  pl.BlockSpec((seq, LANES), lambda b, i, p: (b, cb + p)),
                  pl.BlockSpec((seq, LANES), lambda b, i, p: (b, 2 * cb + p)),
                  pl.BlockSpec((t, idx_w), lambda b, i, p: (b * nq + i, 3 * d_att // idx_w)),
                  pl.BlockSpec((seq, LANES), lambda b, i, p: (b, kwb)),
                  pl.BlockSpec((t, LANES), lambda b, i, p: (b * nq + i, kwb)),
                  pl.BlockSpec((2, 2, t, t), lambda b, i, p: (p, 0, 0, 0))],
        out_specs=pl.BlockSpec((t, LANES), lambda b, i, p: (b * nq + i, p)),
        out_shape=jax.ShapeDtypeStruct((batch * seq, d_att), _BF16),
        scratch_shapes=[pltpu.VMEM((seq, IDX_DIM), _BF16),
                        pltpu.VMEM((nq, t, t), jnp.int32),
                        pltpu.VMEM((nq, t, t), _F32),
                        pltpu.VMEM((2, t, t), _F32),
                        pltpu.VMEM((2, nq, t, t), _F32),
                        pltpu.VMEM((2, 2, t, t), _F32),
                        pltpu.VMEM((2, LANES, t), _F32),
                        pltpu.VMEM((1, t), jnp.int32)],
        compiler_params=_params("parallel", "arbitrary", "arbitrary"),
        name="dsa_attention",
    )(proj2, proj2, proj2, proj2, proj1, proj1, tb)


def _layer_norm(y, g, b):
    mu = jnp.mean(y, axis=-1, keepdims=True)
    d = y - mu
    var = jnp.mean(d * d, axis=-1, keepdims=True)
    return d * lax.rsqrt(var + LN_EPS) * g + b


def _outproj_kernel(c_ref, a_ref, w1_ref, w2_ref, b_ref, x_ref, g_ref, beta_ref, o_ref, ob_ref, ot_ref, *, alpha):
    mix = jnp.dot(c_ref[...], w1_ref[...], preferred_element_type=_F32)
    mix = mix + jnp.dot(a_ref[...], w2_ref[...], preferred_element_type=_F32) + b_ref[...]
    y = _layer_norm(alpha * x_ref[...] + mix, g_ref[...], beta_ref[...])
    o_ref[...] = y
    ob_ref[...] = y.astype(_BF16)
    ot_ref[...] = y.T.astype(_BF16)


def _out_projection(conv_out, att_out, w_out, b_out, x2, g, beta, alpha, tm):
    n, d = x2.shape
    dc = conv_out.shape[1]
    da = att_out.shape[1]
    vec = pl.BlockSpec((1, d), lambda i: (0, 0))
    return pl.pallas_call(
        functools.partial(_outproj_kernel, alpha=alpha),
        grid=(n // tm,),
        in_specs=[pl.BlockSpec((tm, dc), lambda i: (i, 0)),
                  pl.BlockSpec((tm, da), lambda i: (i, 0)),
                  pl.BlockSpec((dc, d), lambda i: (0, 0)),
                  pl.BlockSpec((da, d), lambda i: (0, 0)),
                  vec,
                  pl.BlockSpec((tm, d), lambda i: (i, 0)),
                  vec, vec],
        out_specs=[pl.BlockSpec((tm, d), lambda i: (i, 0)), pl.BlockSpec((tm, d), lambda i: (i, 0)),
                   pl.BlockSpec((d, tm), lambda i: (0, i))],
        out_shape=[jax.ShapeDtypeStruct((n, d), _F32), jax.ShapeDtypeStruct((n, d), _BF16),
                   jax.ShapeDtypeStruct((d, n), _BF16)],
        compiler_params=_params("parallel"),
        name="out_projection_ln1",
    )(conv_out, att_out, w_out[:dc], w_out[dc:], b_out.reshape(1, d), x2, g.reshape(1, d), beta.reshape(1, d))


def _peer_pairs(k):
    return [(i, j) for i in range(k) for j in range(k) if (i + 1) * (j + 1) <= k]


PEER_POPS = PEER_TOPK + 1


def _batcher_pairs(n):
    pairs = []
    p = 1
    while p < n:
        k = p
        while k >= 1:
            for j in range(k % p, n - k, 2 * k):
                for i in range(min(k, n - j - k)):
                    if (i + j) // (2 * p) == (i + j + k) // (2 * p):
                        pairs.append((i + j, i + j + k))
            k //= 2
        p *= 2
    return pairs


def _sort_desc(xs):
    xs = list(xs)
    for i, j in _batcher_pairs(len(xs)):
        xs[i], xs[j] = jnp.maximum(xs[i], xs[j]), jnp.minimum(xs[i], xs[j])
    return xs


def _bitonic_merge_desc(xs):
    xs = list(xs)
    n = len(xs)
    k = n // 2
    while k >= 1:
        for i in range(n):
            if i & k == 0:
                xs[i], xs[i + k] = jnp.maximum(xs[i], xs[i + k]), jnp.minimum(xs[i], xs[i + k])
        k //= 2
    return xs


def _partner(xs, shift):
    return [pltpu.roll(x, shift, 0) for x in xs]


def _merge_keep(a, b, dropped):
    n = len(a)
    hi = [jnp.maximum(a[i], b[n - 1 - i]) for i in range(n)]
    for i in range(n):
        dropped = jnp.maximum(dropped, jnp.minimum(a[i], b[n - 1 - i]))
    return _bitonic_merge_desc(hi), dropped


def _top_across_sublanes(xs, keep):
    dropped = jnp.full(xs[0].shape, -jnp.inf, _F32)
    for shift in (4, 2, 1):
        other = _partner(xs, shift)
        if 2 * len(xs) <= keep:
            xs = _bitonic_merge_desc(xs + other[::-1])
        else:
            assert len(xs) == keep
            xs, dropped = _merge_keep(xs, other, jnp.maximum(dropped, pltpu.roll(dropped, shift, 0)))
    return xs, dropped


def _pack_sublanes(slabs, sub):
    out = []
    for g in range(-(-len(slabs) // SUBLANES)):
        acc = jnp.full(slabs[0].shape, -jnp.inf, _F32)
        for s in range(SUBLANES):
            if g * SUBLANES + s < len(slabs):
                acc = jnp.where(sub == s, slabs[g * SUBLANES + s], acc)
        out.append(acc)
    return out


def _peer_select_kernel(x_ref, wq_ref, sk_ref, s2_ref, t_ref, e1_ref, e2_ref, *, n_heads, n_keys):
    tm = x_ref.shape[0]
    dk = sk_ref.shape[2]
    q = jnp.dot(x_ref[...], wq_ref[...], preferred_element_type=_F32).astype(_BF16)
    sub = lax.broadcasted_iota(jnp.int32, (SUBLANES, tm), 0)
    pairs = _peer_pairs(PEER_POPS)
    for h in range(n_heads):
        tops = []
        scores = []
        for c in range(2):
            g = 2 * h + c
            s_t = lax.dot_general(sk_ref[g], q[:, g * dk:(g + 1) * dk], _NT,
                                  preferred_element_type=_F32)
            scores.append(s_t)
            groups = _sort_desc([s_t[r * SUBLANES:(r + 1) * SUBLANES, :] for r in range(n_keys // SUBLANES)])
            top, nxt = _top_across_sublanes(groups, PEER_TOPK)
            tops.append(top + [nxt])
        a, b = tops
        cand = _pack_sublanes([a[i] + b[j] for (i, j) in pairs], sub)
        cand += [jnp.full((SUBLANES, tm), -jnp.inf, _F32)] * (SUBLANES - len(cand))
        top, nxt = _top_across_sublanes(_sort_desc(cand), PEER_TOPK)
        vals = [v[0:1, :] for v in top + [nxt]]
        a = [v[0:1, :] for v in a]
        b = [v[0:1, :] for v in b]
        z = jnp.zeros((1, tm), _F32)
        for v in vals[:PEER_TOPK]:
            z = z + jnp.exp(v - vals[0])
        theta = 0.5 * (vals[PEER_TOPK - 1] + vals[PEER_TOPK])
        s2_ref[h] = scores[1]
        t_ref[h] = theta - scores[0]
        e1_ref[h] = jnp.exp(scores[0] - a[0])
        e2_ref[h] = jnp.exp(scores[1] - b[0]) / z


def _peer_select(x1b, wq, sub_keys, tm):
    n, d = x1b.shape
    n_heads, _, n_keys, dk = sub_keys.shape
    assert n_keys == PEER_TOPK * SUBLANES
    sk = sub_keys.reshape(2 * n_heads, n_keys, dk).astype(_BF16)
    g2 = 2 * n_heads
    out = pl.BlockSpec((n_heads, n_keys, tm), lambda i: (0, 0, i))
    shape = jax.ShapeDtypeStruct((n_heads, n_keys, n), _F32)
    return pl.pallas_call(
        functools.partial(_peer_select_kernel, n_heads=n_heads, n_keys=n_keys),
        grid=(n // tm,),
        in_specs=[pl.BlockSpec((tm, d), lambda i: (i, 0)),
                  pl.BlockSpec((d, g2 * dk), lambda i: (0, 0)),
                  pl.BlockSpec((g2, n_keys, dk), lambda i: (0, 0, 0))],
        out_specs=[out, out, out, out],
        out_shape=[shape, shape, shape, shape],
        compiler_params=_params("parallel"),
        name="peer_select",
    )(x1b, wq, sk)


PEER_SUB = 256
PEER_UNIT = 128
PEER_PIECE = 512


def _peer_expert_kernel(xt_ref, u_ref, vt_ref, s2_ref, t_ref, e1_ref, e2_ref, x_ref, g_ref, beta_ref, o_ref,
                        y_ref, a0_ref, a1_ref, *, n_heads, n_keys, alpha, nk):
    k = pl.program_id(1)
    te = u_ref.shape[0]
    tm = xt_ref.shape[1]
    d = vt_ref.shape[0]
    kt = k
    units = [(r, c) for r in range(te // PEER_UNIT) for c in range(tm // PEER_SUB)]
    n_pieces = d // PEER_PIECE
    pieces_of = [[m for m in range(n_pieces) if m * len(units) // n_pieces == ui] for ui in range(len(units))]

    def body(a_cur, a_prev, first=True, second=True):
        def second_matmul_piece(m):
            rows = slice(m * PEER_PIECE, (m + 1) * PEER_PIECE)
            y_ref[rows, :] += jnp.dot(vt_ref[rows, :], a_prev[...], preferred_element_type=_F32)

        for ui, (r2, c2) in enumerate(units):
            if first:
                h_t = jnp.dot(u_ref[r2 * PEER_UNIT:(r2 + 1) * PEER_UNIT, :],
                              xt_ref[:, c2 * PEER_SUB:(c2 + 1) * PEER_SUB],
                              preferred_element_type=_F32)
            if second:
                for m in pieces_of[ui]:
                    second_matmul_piece(m)
            if not first:
                continue
            act = 0.5 * h_t * (1.0 + lax.erf(h_t * (2.0 ** -0.5)))
            for r in range(PEER_UNIT // n_keys):
                blk = r2 * (PEER_UNIT // n_keys) + r
                i1 = kt * (te // n_keys) + blk
                rows = slice(blk * n_keys, (blk + 1) * n_keys)
                for c in range(PEER_SUB // LANES):
                    cols = slice(c2 * PEER_SUB + c * LANES, c2 * PEER_SUB + (c + 1) * LANES)
                    gate = jnp.zeros((n_keys, LANES), _F32)
                    for h in range(n_heads):
                        sel = s2_ref[h, :, cols] >= t_ref[h, pl.ds(i1, 1), :][:, cols]
                        gate = gate + jnp.where(sel, e2_ref[h, :, cols] * e1_ref[h, pl.ds(i1, 1), :][:, cols],
                                                0.0)
                    a_cur[rows, cols] = (gate * act[r * n_keys:(r + 1) * n_keys,
                                                    c * LANES:(c + 1) * LANES]).astype(a_cur.dtype)

    bufs = (a0_ref, a1_ref)

    @pl.when(k == 0)
    def _():
        y_ref[...] = jnp.zeros_like(y_ref)
        body(bufs[0], bufs[1], second=False)

    for parity in range(2):
        @pl.when((k > 0) & (k < nk) & (k % 2 == parity))
        def _():
            body(bufs[parity], bufs[1 - parity])

    @pl.when(k == nk)
    def _():
        body(bufs[nk % 2], bufs[(nk - 1) % 2], first=False)
        y = alpha * x_ref[...] + y_ref[...].T
        o_ref[...] = _layer_norm(y, g_ref[...], beta_ref[...])


def _peer_experts(x1, x1t, u_b, v_tab, s2, t, e1, e2, g, beta, alpha, tm, te):
    n, d = x1.shape
    n_exp = u_b.shape[0]
    n_heads, n_keys, _ = s2.shape
    assert n_exp == n_keys * n_keys and te % PEER_UNIT == 0 and PEER_UNIT % n_keys == 0 and d % PEER_PIECE == 0
    assert n_exp % te == 0 and n % tm == 0 and tm % PEER_SUB == 0
    nk = n_exp // te
    vt_b = jnp.transpose(v_tab.reshape(nk, te, d), (0, 2, 1)).astype(_BF16)
    vec = pl.BlockSpec((1, d), lambda j, k: (0, 0))
    sel = pl.BlockSpec((n_heads, n_keys, tm), lambda j, k: (0, 0, j))
    once = dict(pipeline_mode=pl.Buffered(1))
    return pl.pallas_call(
        functools.partial(_peer_expert_kernel, n_heads=n_heads, n_keys=n_keys, alpha=alpha, nk=nk),
        grid=(n // tm, nk + 1),
        in_specs=[pl.BlockSpec((d, tm), lambda j, k: (0, j)),
                  pl.BlockSpec((te, d), lambda j, k: (jnp.minimum(k, nk - 1), 0)),
                  pl.BlockSpec((None, d, te), lambda j, k: (jnp.maximum(k - 1, 0), 0, 0)),
                  sel, sel, sel, sel,
                  pl.BlockSpec((tm, d), lambda j, k: (j, 0), **once),
                  vec, vec],
        out_specs=pl.BlockSpec((tm, d), lambda j, k: (j, 0), **once),
        out_shape=jax.ShapeDtypeStruct((n, d), _F32),
        scratch_shapes=[pltpu.VMEM((d, tm), _F32), pltpu.VMEM((te, tm), _BF16), pltpu.VMEM((te, tm), _BF16)],
        compiler_params=_params("parallel", "arbitrary"),
        name="peer_experts_ln2",
    )(x1t, u_b, vt_b, s2, t, e1, e2, x1, g.reshape(1, d), beta.reshape(1, d))


def _pick(n, pref):
    t = min(pref, n)
    while n % t:
        t //= 2
    return t


def kernel(x, w_in, w_out, b_out, dw_w, dw_b, conv_ln_g, conv_ln_b, rel_bias, ln1_g, ln1_b,
           peer_wq, peer_sub_keys, peer_u, peer_v, ln2_g, ln2_b):
    batch, seq, d = x.shape
    depth = w_in.shape[0]
    alpha = (2.0 * depth) ** 0.25
    d_conv = dw_w.shape[2]
    d_att = d - d_conv
    d_in = w_in.shape[2]
    n_idx_heads = (d_in - 2 * d_conv - 3 * d_att - IDX_DIM) // (IDX_DIM + 1)
    idx_w = n_idx_heads * IDX_DIM
    assert 2 * d_conv + 3 * d_att + idx_w + IDX_DIM + n_idx_heads == d_in
    n = batch * seq
    x2 = x.reshape(n, d)
    for l in range(depth):
        c0 = 2 * d_conv
        c1 = c0 + 3 * d_att + idx_w
        tail = jnp.pad(w_in[l][:, c1:], ((0, 0), (0, LANES - (d_in - c1))))
        w1 = jnp.concatenate([w_in[l][:, :c0], tail], axis=1).astype(_BF16)
        w2 = w_in[l][:, c0:c1].astype(_BF16)
        tm = _pick(n, 1024)
        proj1 = _matmul(x2, w1, _F32, tm, w1.shape[1])
        proj2 = _matmul(x2, w2, _BF16, tm, _pick(w2.shape[1], 1024))
        conv_out = _conv_group(proj1, dw_w[l], dw_b[l], conv_ln_g[l], conv_ln_b[l], batch, seq, d_conv,
                               _pick(seq, 256))
        att_out = _dsa_attention(proj2, proj1, rel_bias, batch, seq, d_att, d_conv, n_idx_heads)
        x1, x1b, x1t = _out_projection(conv_out, att_out, w_out[l].astype(_BF16), b_out[l], x2,
                                       ln1_g[l], ln1_b[l], alpha, _pick(n, 512))
        s2, t, e1, e2 = _peer_select(x1b, peer_wq[l].astype(_BF16), peer_sub_keys[l], _pick(n, 256))
        x2 = _peer_experts(x1, x1t, peer_u[l].astype(_BF16), peer_v[l], s2, t, e1, e2,
                           ln2_g[l], ln2_b[l], alpha, _pick(n, 512), 512)
    return x2.reshape(batch, seq, d)
```

```python
import functools
import math

import numpy as np
import jax
import jax.numpy as jnp
from jax import lax
from jax.experimental import pallas as pl
from jax.experimental.pallas import tpu as pltpu

HEAD_DIM = 64
IDX_DIM = 64
TOPK_MAX = 256
REL_MAX_DIST = 128
PEER_TOPK = 16
LN_EPS = 1e-5

LANES = 128
SUBLANES = 8
VMEM_LIMIT_BYTES = 56 * 1024 * 1024
NEG_BIG = -1e30
INT_MIN = -(2 ** 31)

_BF16 = jnp.bfloat16
_F32 = jnp.float32
_NT = (((1,), (1,)), ((), ()))
_TN = (((0,), (0,)), ((), ()))


def _params(*sem, flags=None):
    return pltpu.CompilerParams(dimension_semantics=sem, vmem_limit_bytes=VMEM_LIMIT_BYTES, flags=flags)


def _matmul_kernel(x_ref, w_ref, o_ref, xb_ref):
    @pl.when(pl.program_id(1) == 0)
    def _():
        xb_ref[...] = x_ref[...].astype(_BF16)

    o_ref[...] = jnp.dot(xb_ref[...], w_ref[...], preferred_element_type=_F32).astype(o_ref.dtype)


def _matmul(x, w, out_dtype, tm, tn):
    m, k = x.shape
    n = w.shape[1]
    assert m % tm == 0 and n % tn == 0
    return pl.pallas_call(
        _matmul_kernel,
        grid=(m // tm, n // tn),
        in_specs=[pl.BlockSpec((tm, k), lambda i, j: (i, 0)),
                  pl.BlockSpec((k, tn), lambda i, j: (0, j))],
        out_specs=pl.BlockSpec((tm, tn), lambda i, j: (i, j)),
        out_shape=jax.ShapeDtypeStruct((m, n), out_dtype),
        scratch_shapes=[pltpu.VMEM((tm, k), _BF16)],
        compiler_params=_params("parallel", "arbitrary"),
        name="proj_matmul",
    )(x, w)


CONV_HALO = 32
CONV_ROWS = 16
CONV_UNROLL = 4


def _conv_kernel(a1_ref, a2_ref, h1_ref, h2_ref, w_ref, b_ref, g_ref, beta_ref, o_ref, hs_ref, rot_ref,
                 *, t_rows, width):
    first = pl.program_id(1) == 0
    halo = h1_ref[...] * jax.nn.sigmoid(h2_ref[...])
    hs_ref[0:CONV_HALO, :] = jnp.where(first, 0.0, halo)
    hs_ref[CONV_HALO:CONV_HALO + t_rows, :] = a1_ref[...] * jax.nn.sigmoid(a2_ref[...])
    n_rot = t_rows + CONV_HALO - SUBLANES
    rot_ref[0, :, :] = hs_ref[...]
    for r in range(1, SUBLANES):
        rot_ref[r, 0:n_rot, :] = hs_ref[r:r + n_rot, :]
    base = CONV_HALO - (width - 1)
    inv_c = 1.0 / o_ref.shape[-1]

    def chunk(c, carry):
        row0 = pl.multiple_of(c * CONV_ROWS, CONV_ROWS)
        acc = jnp.broadcast_to(b_ref[...], (CONV_ROWS, o_ref.shape[-1]))
        for j in range(width):
            q, r = divmod(base + j, SUBLANES)
            tap = rot_ref[r, pl.ds(row0 + q * SUBLANES, CONV_ROWS), :]
            acc = acc + w_ref[j:j + 1, :] * tap
        mu = jnp.sum(acc, axis=-1, keepdims=True) * inv_c
        d = acc - mu
        var = jnp.sum(d * d, axis=-1, keepdims=True) * inv_c
        y = d * lax.rsqrt(var + LN_EPS) * g_ref[...] + beta_ref[...]
        o_ref[pl.ds(row0, CONV_ROWS), :] = (y * jax.nn.sigmoid(y)).astype(o_ref.dtype)
        return carry

    lax.fori_loop(0, t_rows // CONV_ROWS, chunk, 0, unroll=CONV_UNROLL)


def _conv_group(proj1, dw_w, dw_b, cn_g, cn_b, batch, seq, d_conv, t_rows):
    width = dw_w.shape[0]
    assert width - 1 <= CONV_HALO and seq % t_rows == 0 and t_rows % CONV_HALO == 0
    nt = seq // t_rows
    hb = t_rows // CONV_HALO

    def cur(col):
        return pl.BlockSpec((t_rows, d_conv), lambda b, s: (b * nt + s, col))

    def halo(col):
        return pl.BlockSpec((CONV_HALO, d_conv), lambda b, s: (jnp.maximum((b * nt + s) * hb - 1, 0), col))

    vec = pl.BlockSpec((1, d_conv), lambda b, s: (0, 0))
    return pl.pallas_call(
        functools.partial(_conv_kernel, t_rows=t_rows, width=width),
        grid=(batch, nt),
        in_specs=[cur(0), cur(1), halo(0), halo(1),
                  pl.BlockSpec((width, d_conv), lambda b, s: (0, 0)), vec, vec, vec],
        out_specs=pl.BlockSpec((t_rows, d_conv), lambda b, s: (b * nt + s, 0)),
        out_shape=jax.ShapeDtypeStruct((batch * seq, d_conv), _BF16),
        scratch_shapes=[pltpu.VMEM((t_rows + CONV_HALO, d_conv), _F32),
                        pltpu.VMEM((SUBLANES, t_rows + CONV_HALO, d_conv), _F32)],
        compiler_params=_params("parallel", "arbitrary"),
        name="conv_group",
    )(proj1, proj1, proj1, proj1, dw_w, dw_b.reshape(1, -1), cn_g.reshape(1, -1), cn_b.reshape(1, -1))


ATT_TILE = 256
ATT_CHUNK = 4


def _sortable_key(x):
    b = pltpu.bitcast(x, jnp.int32)
    return b ^ ((b >> 31) & jnp.int32(0x7FFFFFFF))


def _fold8(x, op):
    r, c = x.shape
    x = x.reshape(r // SUBLANES, SUBLANES, c)
    return jnp.sum(x, axis=0) if op == "sum" else jnp.max(x, axis=0)


def _dsa_kernel(q_ref, k_ref, v_ref, qi_ref, kw_ref, kwq_ref, tb_ref, o_ref,
                kib_ref, keys_ref, maskb_ref, maskn_ref, lg_ref, lgn_ref, acc_ref, cut_ref,
                *, n_idx_heads, n_sel, n_pos):
    t = ATT_TILE
    i = pl.program_id(1)
    p = pl.program_id(2)
    near_base = jnp.maximum(i - 1, 0)
    n_far = near_base
    n_far_chunks = lax.shift_right_logical(n_far + (ATT_CHUNK - 1), ATT_CHUNK.bit_length() - 1)
    row = lax.broadcasted_iota(jnp.int32, (t, t), 0)
    col = lax.broadcasted_iota(jnp.int32, (t, t), 1)

    def causal(kt):
        return (kt * t + row) <= (i * t + col)

    @pl.when(p == 0)
    def _select():
        @pl.when(i == 0)
        def _():
            kib_ref[...] = kw_ref[:, 0:IDX_DIM].astype(_BF16)

        w_t = kwq_ref[...].T

        def score_tile(kt, carry):
            ki = kib_ref[pl.ds(pl.multiple_of(kt * t, t), t), :]
            acc = jnp.zeros((t, t), _F32)
            for h in range(n_idx_heads):
                z = lax.dot_general(ki, qi_ref[:, h * IDX_DIM:(h + 1) * IDX_DIM], _NT,
                                    preferred_element_type=_F32)
                acc = acc + jnp.maximum(z, 0.0) * w_t[IDX_DIM + h:IDX_DIM + h + 1, :]
            acc = jnp.where(causal(kt), acc, -jnp.inf)
            keys_ref[kt] = _sortable_key(acc)
            return carry

        lax.fori_loop(0, i + 1, score_tile, 0)

        def count_ge(cand):
            def body(kt, pc):
                return pc + _fold8((keys_ref[kt] >= cand).astype(jnp.int32), "sum")
            pc = lax.fori_loop(0, i + 1, body, jnp.zeros((SUBLANES, t), jnp.int32))
            return jnp.sum(pc, axis=0, keepdims=True)

        zero = jnp.zeros((1, t), jnp.int32)
        tau0 = jnp.where(count_ge(zero) >= n_sel, zero, jnp.int32(INT_MIN))

        def bit_step(it, tau):
            cand = tau + (jnp.int32(1) << (30 - it))
            return jnp.where(count_ge(cand) >= n_sel, cand, tau)

        tau = lax.fori_loop(0, 31, bit_step, tau0)

        n_gt = count_ge(tau + 1)
        need = n_sel - n_gt
        excess = count_ge(tau) - n_gt - need
        cut_ref[...] = jnp.full((1, t), n_pos, jnp.int32)

        @pl.when(jnp.max(excess) > 0)
        def _():
            def count_eq_below(bound):
                def body(kt, pc):
                    hit = (keys_ref[kt] == tau) & ((kt * t + row) < bound)
                    return pc + _fold8(hit.astype(jnp.int32), "sum")
                pc = lax.fori_loop(0, i + 1, body, jnp.zeros((SUBLANES, t), jnp.int32))
                return jnp.sum(pc, axis=0, keepdims=True)

            def pos_step(it, pos):
                cand = pos + (jnp.int32(n_pos) >> (it + 1))
                return jnp.where(count_eq_below(cand) < need, cand, pos)

            cut_ref[...] = lax.fori_loop(0, n_pos.bit_length() - 1, pos_step, jnp.zeros((1, t), jnp.int32))

        cut = cut_ref[...]

        def mask_tile(kt, carry):
            key = keys_ref[kt]
            sel = ((key > tau) | ((key == tau) & ((kt * t + row) <= cut))) & causal(kt)
            maskb_ref[kt] = jnp.where(sel, 0.0, NEG_BIG)
            return carry

        lax.fori_loop(0, i + 1, mask_tile, 0)
        neg_tile = jnp.full((t, t), NEG_BIG, _F32)
        maskn_ref[0] = maskb_ref[near_base]
        maskn_ref[1] = jnp.where(i > 0, maskb_ref[i], neg_tile)

        def pad_tile(kt, carry):
            maskb_ref[kt] = neg_tile
            return carry

        lax.fori_loop(n_far, n_far_chunks * ATT_CHUNK, pad_tile, 0)

    ck = ATT_CHUNK * t
    lane = lax.broadcasted_iota(jnp.int32, (t, LANES), 1)
    q_all = q_ref[...] * jnp.asarray(HEAD_DIM ** -0.5, _BF16)
    qm = [jnp.where((lane >= hl * HEAD_DIM) & (lane < (hl + 1) * HEAD_DIM), q_all, jnp.zeros_like(q_all))
          for hl in range(2)]
    near_rows = pl.ds(pl.multiple_of(near_base * t, t), 2 * t)
    near_bias = (jnp.where(i > 0, 0, 1), 1)

    def far_logits(c, mps):
        kk = k_ref[pl.ds(pl.multiple_of(c * ck, ck), ck), :]
        new = []
        for hl in range(2):
            st = lax.dot_general(kk, qm[hl], _NT, preferred_element_type=_F32)
            mp = mps[hl]
            for u in range(ATT_CHUNK):
                tile = st[u * t:(u + 1) * t, :] + maskb_ref[c * ATT_CHUNK + u]
                lg_ref[hl, c * ATT_CHUNK + u] = tile
                mp = jnp.maximum(mp, _fold8(tile, "max"))
            new.append(mp)
        return tuple(new)

    mp0 = jnp.full((SUBLANES, t), NEG_BIG, _F32)
    mps = lax.fori_loop(0, n_far_chunks, far_logits, (mp0, mp0))
    kk = k_ref[near_rows, :]
    m = []
    for hl in range(2):
        st = lax.dot_general(kk, qm[hl], _NT, preferred_element_type=_F32)
        mp = mps[hl]
        for s in range(2):
            tile = st[s * t:(s + 1) * t, :] + maskn_ref[s] + tb_ref[hl, near_bias[s]]
            lgn_ref[hl, s] = tile
            mp = jnp.maximum(mp, _fold8(tile, "max"))
        m.append(jnp.max(mp, axis=0, keepdims=True))

    acc_ref[...] = jnp.zeros_like(acc_ref)

    def far_pv(c, lps):
        vv = v_ref[pl.ds(pl.multiple_of(c * ck, ck), ck), :]
        new = []
        for hl in range(2):
            pr = jnp.exp(lg_ref[hl, pl.ds(c * ATT_CHUNK, ATT_CHUNK)].reshape(ck, t) - m[hl])
            new.append(lps[hl] + _fold8(pr, "sum"))
            acc_ref[hl] += lax.dot_general(vv, pr.astype(_BF16), _TN, preferred_element_type=_F32)
        return tuple(new)

    lp0 = jnp.zeros((SUBLANES, t), _F32)
    lps = lax.fori_loop(0, n_far_chunks, far_pv, (lp0, lp0))
    vv = v_ref[near_rows, :]
    outs = []
    for hl in range(2):
        pr = jnp.exp(lgn_ref[hl].reshape(2 * t, t) - m[hl])
        lp = lps[hl] + _fold8(pr, "sum")
        o_t = acc_ref[hl] + lax.dot_general(vv, pr.astype(_BF16), _TN, preferred_element_type=_F32)
        o_t = o_t / jnp.sum(lp, axis=0, keepdims=True)
        outs.append(o_t[hl * HEAD_DIM:(hl + 1) * HEAD_DIM, :])
    o_ref[...] = jnp.concatenate(outs, axis=0).T.astype(o_ref.dtype)


def _rel_bucket_table(n_buckets):
    d = np.arange(REL_MAX_DIST, dtype=np.int32)
    max_exact = n_buckets // 2
    ratio = np.log(np.maximum(d, 1).astype(np.float32) / np.float32(max_exact)) / np.float32(
        math.log(REL_MAX_DIST / max_exact)) * np.float32(n_buckets - max_exact)
    large = np.minimum(max_exact + ratio.astype(np.int32), n_buckets - 1)
    return np.where(d < max_exact, d, large)


def _band_bias(rel_bias):
    n_buckets, n_heads = rel_bias.shape
    t = ATT_TILE
    assert REL_MAX_DIST <= t
    rb = rel_bias.astype(_F32)
    shifted = rb - rb[n_buckets - 1][None, :]
    by_dist = shifted[_rel_bucket_table(n_buckets)].T
    dv = jnp.concatenate([jnp.zeros((n_heads, t - 1), _F32), by_dist,
                          jnp.zeros((n_heads, 2 * t - REL_MAX_DIST + 1), _F32)], axis=1)
    slots = []
    for off in (t, 0):
        v = dv[:, off:off + 2 * t]
        flat = jnp.tile(v, (1, t))[:, t - 1:t - 1 + t * (2 * t - 1)]
        slots.append(flat.reshape(n_heads, t, 2 * t - 1)[:, :, :t])
    return jnp.stack(slots, axis=1)


def _dsa_attention(proj2, proj1, rel_bias, batch, seq, d_att, d_conv, n_idx_heads):
    t = ATT_TILE
    n_heads = d_att // HEAD_DIM
    n_pairs = n_heads // 2
    idx_w = n_idx_heads * IDX_DIM
    assert seq % t == 0 and n_heads % 2 == 0 and (3 * d_att) % idx_w == 0 and (2 * d_conv) % LANES == 0
    assert IDX_DIM + n_idx_heads <= LANES
    nq = seq // t
    n_sel = min(TOPK_MAX, seq // 4)
    assert n_sel <= t and seq & (seq - 1) == 0 and nq >= 2 and nq % ATT_CHUNK == 0
    cb = d_att // LANES
    kwb = 2 * d_conv // LANES
    tb = _band_bias(rel_bias)
    return pl.pallas_call(
        functools.partial(_dsa_kernel, n_idx_heads=n_idx_heads, n_sel=n_sel, n_pos=seq),
        grid=(batch, nq, n_pairs),
        in_specs=[pl.BlockSpec((t, LANES), lambda b, i, p: (b * nq + i, p)),
                  pl.BlockSpec((seq, LANES), lambda b, i, p: (b, cb + p)),
                  pl.BlockSpec((seq, LANES), lambda b, i, p: (b, 2 * cb + p)),
                  pl.BlockSpec((t, idx_w), lambda b, i, p: (b * nq + i, 3 * d_att // idx_w)),
                  pl.BlockSpec((seq, LANES), lambda b, i, p: (b, kwb)),
                  pl.BlockSpec((t, LANES), lambda b, i, p: (b * nq + i, kwb)),
                  pl.BlockSpec((2, 2, t, t), lambda b, i, p: (p, 0, 0, 0))],
        out_specs=pl.BlockSpec((t, LANES), lambda b, i, p: (b * nq + i, p)),
        out_shape=jax.ShapeDtypeStruct((batch * seq, d_att), _BF16),
        scratch_shapes=[pltpu.VMEM((seq, IDX_DIM), _BF16),
                        pltpu.VMEM((nq, t, t), jnp.int32),
                        pltpu.VMEM((nq, t, t), _F32),
                        pltpu.VMEM((2, t, t), _F32),
                        pltpu.VMEM((2, nq, t, t), _F32),
                        pltpu.VMEM((2, 2, t, t), _F32),
                        pltpu.VMEM((2, LANES, t), _F32),
                        pltpu.VMEM((1, t), jnp.int32)],
        compiler_params=_params("parallel", "arbitrary", "arbitrary"),
        name="dsa_attention",
    )(proj2, proj2, proj2, proj2, proj1, proj1, tb)


def _layer_norm(y, g, b):
    mu = jnp.mean(y, axis=-1, keepdims=True)
    d = y - mu
    var = jnp.mean(d * d, axis=-1, keepdims=True)
    return d * lax.rsqrt(var + LN_EPS) * g + b


def _outproj_kernel(c_ref, a_ref, w1_ref, w2_ref, b_ref, x_ref, g_ref, beta_ref, o_ref, ob_ref, ot_ref, *, alpha):
    mix = jnp.dot(c_ref[...], w1_ref[...], preferred_element_type=_F32)
    mix = mix + jnp.dot(a_ref[...], w2_ref[...], preferred_element_type=_F32) + b_ref[...]
    y = _layer_norm(alpha * x_ref[...] + mix, g_ref[...], beta_ref[...])
    o_ref[...] = y
    ob_ref[...] = y.astype(_BF16)
    ot_ref[...] = y.T.astype(_BF16)


def _out_projection(conv_out, att_out, w_out, b_out, x2, g, beta, alpha, tm):
    n, d = x2.shape
    dc = conv_out.shape[1]
    da = att_out.shape[1]
    vec = pl.BlockSpec((1, d), lambda i: (0, 0))
    return pl.pallas_call(
        functools.partial(_outproj_kernel, alpha=alpha),
        grid=(n // tm,),
        in_specs=[pl.BlockSpec((tm, dc), lambda i: (i, 0)),
                  pl.BlockSpec((tm, da), lambda i: (i, 0)),
                  pl.BlockSpec((dc, d), lambda i: (0, 0)),
                  pl.BlockSpec((da, d), lambda i: (0, 0)),
                  vec,
                  pl.BlockSpec((tm, d), lambda i: (i, 0)),
                  vec, vec],
        out_specs=[pl.BlockSpec((tm, d), lambda i: (i, 0)), pl.BlockSpec((tm, d), lambda i: (i, 0)),
                   pl.BlockSpec((d, tm), lambda i: (0, i))],
        out_shape=[jax.ShapeDtypeStruct((n, d), _F32), jax.ShapeDtypeStruct((n, d), _BF16),
                   jax.ShapeDtypeStruct((d, n), _BF16)],
        compiler_params=_params("parallel"),
        name="out_projection_ln1",
    )(conv_out, att_out, w_out[:dc], w_out[dc:], b_out.reshape(1, d), x2, g.reshape(1, d), beta.reshape(1, d))


def _peer_pairs(k):
    return [(i, j) for i in range(k) for j in range(k) if (i + 1) * (j + 1) <= k]


PEER_POPS = PEER_TOPK + 1


def _batcher_pairs(n):
    pairs = []
    p = 1
    while p < n:
        k = p
        while k >= 1:
            for j in range(k % p, n - k, 2 * k):
                for i in range(min(k, n - j - k)):
                    if (i + j) // (2 * p) == (i + j + k) // (2 * p):
                        pairs.append((i + j, i + j + k))
            k //= 2
        p *= 2
    return pairs


def _sort_desc(xs):
    xs = list(xs)
    for i, j in _batcher_pairs(len(xs)):
        xs[i], xs[j] = jnp.maximum(xs[i], xs[j]), jnp.minimum(xs[i], xs[j])
    return xs


def _bitonic_merge_desc(xs):
    xs = list(xs)
    n = len(xs)
    k = n // 2
    while k >= 1:
        for i in range(n):
            if i & k == 0:
                xs[i], xs[i + k] = jnp.maximum(xs[i], xs[i + k]), jnp.minimum(xs[i], xs[i + k])
        k //= 2
    return xs


def _partner(xs, shift):
    return [pltpu.roll(x, shift, 0) for x in xs]


def _merge_keep(a, b, dropped):
    n = len(a)
    hi = [jnp.maximum(a[i], b[n - 1 - i]) for i in range(n)]
    for i in range(n):
        dropped = jnp.maximum(dropped, jnp.minimum(a[i], b[n - 1 - i]))
    return _bitonic_merge_desc(hi), dropped


def _top_across_sublanes(xs, keep):
    dropped = jnp.full(xs[0].shape, -jnp.inf, _F32)
    for shift in (4, 2, 1):
        other = _partner(xs, shift)
        if 2 * len(xs) <= keep:
            xs = _bitonic_merge_desc(xs + other[::-1])
        else:
            assert len(xs) == keep
            xs, dropped = _merge_keep(xs, other, jnp.maximum(dropped, pltpu.roll(dropped, shift, 0)))
    return xs, dropped


def _pack_sublanes(slabs, sub):
    out = []
    for g in range(-(-len(slabs) // SUBLANES)):
        acc = jnp.full(slabs[0].shape, -jnp.inf, _F32)
        for s in range(SUBLANES):
            if g * SUBLANES + s < len(slabs):
                acc = jnp.where(sub == s, slabs[g * SUBLANES + s], acc)
        out.append(acc)
    return out


def _peer_select_kernel(x_ref, wq_ref, sk_ref, s2_ref, t_ref, e1_ref, e2_ref, *, n_heads, n_keys):
    tm = x_ref.shape[0]
    dk = sk_ref.shape[2]
    q = jnp.dot(x_ref[...], wq_ref[...], preferred_element_type=_F32).astype(_BF16)
    sub = lax.broadcasted_iota(jnp.int32, (SUBLANES, LANES), 0)
    pairs = _peer_pairs(PEER_POPS)
    for h in range(n_heads):
        full = [lax.dot_general(sk_ref[2 * h + c], q[:, (2 * h + c) * dk:(2 * h + c + 1) * dk], _NT,
                                preferred_element_type=_F32) for c in range(2)]
        s2_ref[h] = full[1]
        for w in range(tm // LANES):
            cols = slice(w * LANES, (w + 1) * LANES)
            scores = [f[:, cols] for f in full]
            tops = []
            for s_t in scores:
                groups = _sort_desc([s_t[r * SUBLANES:(r + 1) * SUBLANES, :] for r in range(n_keys // SUBLANES)])
                top, nxt = _top_across_sublanes(groups, PEER_TOPK)
                tops.append(top + [nxt])
            a, b = tops
            cand = _pack_sublanes([a[i] + b[j] for (i, j) in pairs], sub)
            cand += [jnp.full((SUBLANES, LANES), -jnp.inf, _F32)] * (SUBLANES - len(cand))
            top, nxt = _top_across_sublanes(_sort_desc(cand), PEER_TOPK)
            vals = [v[0:1, :] for v in top + [nxt]]
            z = jnp.zeros((1, LANES), _F32)
            for v in vals[:PEER_TOPK]:
                z = z + jnp.exp(v - vals[0])
            theta = 0.5 * (vals[PEER_TOPK - 1] + vals[PEER_TOPK])
            t_ref[h, :, cols] = theta - scores[0]
            e1_ref[h, :, cols] = jnp.exp(scores[0] - a[0][0:1, :])
            e2_ref[h, :, cols] = jnp.exp(scores[1] - b[0][0:1, :]) / z


def _peer_select(x1b, wq, sub_keys, tm):
    n, d = x1b.shape
    n_heads, _, n_keys, dk = sub_keys.shape
    assert n_keys == PEER_TOPK * SUBLANES
    sk = sub_keys.reshape(2 * n_heads, n_keys, dk).astype(_BF16)
    g2 = 2 * n_heads
    out = pl.BlockSpec((n_heads, n_keys, tm), lambda i: (0, 0, i))
    shape = jax.ShapeDtypeStruct((n_heads, n_keys, n), _F32)
    return pl.pallas_call(
        functools.partial(_peer_select_kernel, n_heads=n_heads, n_keys=n_keys),
        grid=(n // tm,),
        in_specs=[pl.BlockSpec((tm, d), lambda i: (i, 0)),
                  pl.BlockSpec((d, g2 * dk), lambda i: (0, 0)),
                  pl.BlockSpec((g2, n_keys, dk), lambda i: (0, 0, 0))],
        out_specs=[out, out, out, out],
        out_shape=[shape, shape, shape, shape],
        compiler_params=_params("parallel"),
        name="peer_select",
    )(x1b, wq, sk)


PEER_SUB = 256
PEER_UNIT = 256
PEER_PIECE = 512


def _peer_expert_kernel(xt_ref, u_ref, vt_ref, s2_ref, t_ref, e1_ref, e2_ref, x_ref, g_ref, beta_ref, o_ref,
                        y_ref, a0_ref, a1_ref, *, n_heads, n_keys, alpha, nk):
    k = pl.program_id(1)
    te = u_ref.shape[0]
    tm = xt_ref.shape[1]
    d = vt_ref.shape[0]
    kt = k
    units = [(r, c) for r in range(te // PEER_UNIT) for c in range(tm // PEER_SUB)]
    n_pieces = d // PEER_PIECE
    pieces_of = [[m for m in range(n_pieces) if m * len(units) // n_pieces == ui] for ui in range(len(units))]

    def body(a_cur, a_prev, first=True, second=True):
        def second_matmul_piece(m):
            rows = slice(m * PEER_PIECE, (m + 1) * PEER_PIECE)
            y_ref[rows, :] += jnp.dot(vt_ref[rows, :], a_prev[...], preferred_element_type=_F32)

        for ui, (r2, c2) in enumerate(units):
            if first:
                h_t = jnp.dot(u_ref[r2 * PEER_UNIT:(r2 + 1) * PEER_UNIT, :],
                              xt_ref[:, c2 * PEER_SUB:(c2 + 1) * PEER_SUB],
                              preferred_element_type=_F32)
            if second:
                for m in pieces_of[ui]:
                    second_matmul_piece(m)
            if not first:
                continue
            act = 0.5 * h_t * (1.0 + lax.erf(h_t * (2.0 ** -0.5)))
            for r in range(PEER_UNIT // n_keys):
                blk = r2 * (PEER_UNIT // n_keys) + r
                i1 = kt * (te // n_keys) + blk
                rows = slice(blk * n_keys, (blk + 1) * n_keys)
                for c in range(PEER_SUB // LANES):
                    cols = slice(c2 * PEER_SUB + c * LANES, c2 * PEER_SUB + (c + 1) * LANES)
                    gate = jnp.zeros((n_keys, LANES), _F32)
                    for h in range(n_heads):
                        sel = s2_ref[h, :, cols] >= t_ref[h, pl.ds(i1, 1), :][:, cols]
                        gate = gate + jnp.where(sel, e2_ref[h, :, cols] * e1_ref[h, pl.ds(i1, 1), :][:, cols],
                                                0.0)
                    a_cur[rows, cols] = (gate * act[r * n_keys:(r + 1) * n_keys,
                                                    c * LANES:(c + 1) * LANES]).astype(a_cur.dtype)

    bufs = (a0_ref, a1_ref)

    @pl.when(k == 0)
    def _():
        y_ref[...] = jnp.zeros_like(y_ref)
        body(bufs[0], bufs[1], second=False)

    for parity in range(2):
        @pl.when((k > 0) & (k < nk) & (k % 2 == parity))
        def _():
            body(bufs[parity], bufs[1 - parity])

    @pl.when(k == nk)
    def _():
        body(bufs[nk % 2], bufs[(nk - 1) % 2], first=False)
        y = alpha * x_ref[...] + y_ref[...].T
        o_ref[...] = _layer_norm(y, g_ref[...], beta_ref[...])


def _peer_experts(x1, x1t, u_b, v_tab, s2, t, e1, e2, g, beta, alpha, tm, te):
    n, d = x1.shape
    n_exp = u_b.shape[0]
    n_heads, n_keys, _ = s2.shape
    assert n_exp == n_keys * n_keys and te % PEER_UNIT == 0 and PEER_UNIT % n_keys == 0 and d % PEER_PIECE == 0
    assert n_exp % te == 0 and n % tm == 0 and tm % PEER_SUB == 0
    nk = n_exp // te
    vt_b = jnp.transpose(v_tab.reshape(nk, te, d), (0, 2, 1)).astype(_BF16)
    vec = pl.BlockSpec((1, d), lambda j, k: (0, 0))
    sel = pl.BlockSpec((n_heads, n_keys, tm), lambda j, k: (0, 0, j))
    once = dict(pipeline_mode=pl.Buffered(1))
    return pl.pallas_call(
        functools.partial(_peer_expert_kernel, n_heads=n_heads, n_keys=n_keys, alpha=alpha, nk=nk),
        grid=(n // tm, nk + 1),
        in_specs=[pl.BlockSpec((d, tm), lambda j, k: (0, j)),
                  pl.BlockSpec((te, d), lambda j, k: (jnp.minimum(k, nk - 1), 0)),
                  pl.BlockSpec((None, d, te), lambda j, k: (jnp.maximum(k - 1, 0), 0, 0)),
                  sel, sel, sel, sel,
                  pl.BlockSpec((tm, d), lambda j, k: (j, 0), **once),
                  vec, vec],
        out_specs=pl.BlockSpec((tm, d), lambda j, k: (j, 0), **once),
        out_shape=jax.ShapeDtypeStruct((n, d), _F32),
        scratch_shapes=[pltpu.VMEM((d, tm), _F32), pltpu.VMEM((te, tm), _BF16), pltpu.VMEM((te, tm), _BF16)],
        compiler_params=_params("parallel", "arbitrary"),
        name="peer_experts_ln2",
    )(x1t, u_b, vt_b, s2, t, e1, e2, x1, g.reshape(1, d), beta.reshape(1, d))


def _pick(n, pref):
    t = min(pref, n)
    while n % t:
        t //= 2
    return t


def kernel(x, w_in, w_out, b_out, dw_w, dw_b, conv_ln_g, conv_ln_b, rel_bias, ln1_g, ln1_b,
           peer_wq, peer_sub_keys, peer_u, peer_v, ln2_g, ln2_b):
    batch, seq, d = x.shape
    depth = w_in.shape[0]
    alpha = (2.0 * depth) ** 0.25
    d_conv = dw_w.shape[2]
    d_att = d - d_conv
    d_in = w_in.shape[2]
    n_idx_heads = (d_in - 2 * d_conv - 3 * d_att - IDX_DIM) // (IDX_DIM + 1)
    idx_w = n_idx_heads * IDX_DIM
    assert 2 * d_conv + 3 * d_att + idx_w + IDX_DIM + n_idx_heads == d_in
    n = batch * seq
    x2 = x.reshape(n, d)
    for l in range(depth):
        c0 = 2 * d_conv
        c1 = c0 + 3 * d_att + idx_w
        tail = jnp.pad(w_in[l][:, c1:], ((0, 0), (0, LANES - (d_in - c1))))
        w1 = jnp.concatenate([w_in[l][:, :c0], tail], axis=1).astype(_BF16)
        w2 = w_in[l][:, c0:c1].astype(_BF16)
        tm = _pick(n, 1024)
        proj1 = _matmul(x2, w1, _F32, tm, w1.shape[1])
        proj2 = _matmul(x2, w2, _BF16, tm, _pick(w2.shape[1], 1024))
        conv_out = _conv_group(proj1, dw_w[l], dw_b[l], conv_ln_g[l], conv_ln_b[l], batch, seq, d_conv,
                               _pick(seq, 256))
        att_out = _dsa_attention(proj2, proj1, rel_bias, batch, seq, d_att, d_conv, n_idx_heads)
        x1, x1b, x1t = _out_projection(conv_out, att_out, w_out[l].astype(_BF16), b_out[l], x2,
                                       ln1_g[l], ln1_b[l], alpha, _pick(n, 512))
        s2, t, e1, e2 = _peer_select(x1b, peer_wq[l].astype(_BF16), peer_sub_keys[l], _pick(n, 256))
        x2 = _peer_experts(x1, x1t, peer_u[l].astype(_BF16), peer_v[l], s2, t, e1, e2,
                           ln2_g[l], ln2_b[l], alpha, _pick(n, 512), 512)
    return x2.reshape(batch, seq, d)
```
